```python
import jax, jax.numpy as jnp
from jax import lax
import numpy as np

D_MODEL = 1024
BATCH = 8
SEQ = 8192
DEPTH = 1

CHUNK = 64

MIX_WIDTH = D_MODEL
CONV_W = MIX_WIDTH // 2
LRU_W = MIX_WIDTH - CONV_W
N_CONV_HEADS = 8
N_LRU_HEADS = 8
CONV_HD = CONV_W // N_CONV_HEADS
LRU_HD = LRU_W // N_LRU_HEADS
SHORT_CONV_K = 3
LRU_CONV_K = 4
LRU_C = 8.0
N_IN = 3 * CONV_W + 2 * LRU_W

PEER_HEADS = 8
PEER_TOPK = 16
N_KEYS = 128
N_EXPERTS = N_KEYS * N_KEYS
PEER_DK = 128
PEER_DK_HALF = PEER_DK // 2
PEER_BLOCK = 128

EPS = 1e-6

kernel_name = "hybrid_conv_rglru_peer_adaln_block"


def rmsnorm(x, g):
    x32 = x.astype(jnp.float32)
    y = x32 * lax.rsqrt(jnp.mean(x32 * x32, axis=-1, keepdims=True) + EPS)
    return y.astype(x.dtype) * g


def head_rmsnorm(y, g, n_heads):
    b, s, w = y.shape
    yh = y.reshape(b, s, n_heads, w // n_heads).astype(jnp.float32)
    yh = yh * lax.rsqrt(jnp.mean(yh * yh, axis=-1, keepdims=True) + EPS)
    return yh.reshape(b, s, w).astype(y.dtype) * g


def modulate(h, shift, scale):
    return h * (1.0 + scale[:, None, :]) + shift[:, None, :]


def causal_dwconv(x, w):
    k_w = w.shape[0]
    s = x.shape[1]
    xp = jnp.pad(x, ((0, 0), (k_w - 1, 0), (0, 0)))
    out = xp[:, 0:s] * w[0]
    for k in range(1, k_w):
        out = out + xp[:, k:k + s] * w[k]
    return out


def rg_lru(xr, w_r, b_r, w_i, b_i, lam):
    b, s, _ = xr.shape
    xh = xr.reshape(b, s, N_LRU_HEADS, LRU_HD)
    r = jax.nn.sigmoid(jnp.einsum('bshi,hij->bshj', xh, w_r) + b_r).astype(jnp.float32)
    i = jax.nn.sigmoid(jnp.einsum('bshi,hij->bshj', xh, w_i) + b_i).astype(jnp.float32)
    log_a = -LRU_C * r * jax.nn.softplus(-lam.astype(jnp.float32))
    a = jnp.exp(log_a)
    u = jnp.sqrt(-jnp.expm1(2.0 * log_a)) * (i * xh.astype(jnp.float32))

    def combine(e1, e2):
        a1, b1 = e1
        a2, b2 = e2
        return a1 * a2, a2 * b1 + b2

    _, h = lax.associative_scan(combine, (a, u), axis=1)
    return h.reshape(b, s, LRU_W).astype(xr.dtype)


def peer(h, w_q, sub_keys, expert_u, expert_v):
    b, s, d = h.shape
    t = b * s
    hb = h.reshape(t // PEER_BLOCK, PEER_BLOCK, d)
    kk = PEER_TOPK * PEER_TOPK

    def block(xb):
        p = xb.shape[0]
        q = (xb @ w_q).reshape(p, PEER_HEADS, 2, PEER_DK_HALF)
        sc = jnp.einsum('phcd,hcnd->phcn', q, sub_keys).astype(jnp.float32)
        top_s, top_i = lax.top_k(sc, PEER_TOPK)
        cand_s = (top_s[:, :, 0, :, None] + top_s[:, :, 1, None, :]).reshape(p, PEER_HEADS, kk)
        cand_i = (top_i[:, :, 0, :, None] * N_KEYS + top_i[:, :, 1, None, :]).reshape(p, PEER_HEADS, kk)
        best_s, best_pos = lax.top_k(cand_s, PEER_TOPK)
        idx = jnp.take_along_axis(cand_i, best_pos, axis=-1)
        g = jax.nn.softmax(best_s, axis=-1)
        u_sel = expert_u[idx]
        act = jax.nn.gelu(jnp.einsum('phkd,pd->phk', u_sel, xb), approximate=False)
        coef = (g * act.astype(jnp.float32)).astype(xb.dtype)
        return jnp.einsum('phk,phkd->pd', coef, expert_v[idx])

    return lax.map(block, hb).reshape(b, s, d)


def setup_inputs(seed: int = 0) -> dict:
    key = jax.random.key(seed)
    ks = jax.random.split(key, 24)
    f32 = jnp.float32
    L = DEPTH

    def nrm(k, shape, scale):
        return jax.random.normal(k, shape, f32) * scale

    a8 = jax.random.uniform(ks[13], (L, N_LRU_HEADS, LRU_HD), f32, 0.9, 0.999)
    a_base = a8 ** (1.0 / LRU_C)
    lru_lambda = jnp.log(a_base) - jnp.log1p(-a_base)

    return {
        "x": nrm(ks[0], (BATCH, SEQ, D_MODEL), 1.0),
        "c": nrm(ks[1], (BATCH, D_MODEL), 1.0),
        "w_ada": nrm(ks[2], (L, D_MODEL, 6 * D_MODEL), 0.5 * D_MODEL ** -0.5),
        "b_ada": nrm(ks[3], (L, 6 * D_MODEL), 0.02),
        "norm1_g": 1.0 + nrm(ks[4], (L, D_MODEL), 0.02),
        "w_in": nrm(ks[5], (L, D_MODEL, N_IN), D_MODEL ** -0.5),
        "conv_a_w": nrm(ks[6], (L, SHORT_CONV_K, CONV_W), SHORT_CONV_K ** -0.5),
        "conv_b_w": nrm(ks[7], (L, LRU_CONV_K, LRU_W), LRU_CONV_K ** -0.5),
        "conv_b_b": nrm(ks[8], (L, LRU_W), 0.02),
        "w_r": nrm(ks[9], (L, N_LRU_HEADS, LRU_HD, LRU_HD), LRU_HD ** -0.5),
        "b_r": nrm(ks[10], (L, N_LRU_HEADS, LRU_HD), 0.02),
        "w_i": nrm(ks[11], (L, N_LRU_HEADS, LRU_HD, LRU_HD), LRU_HD ** -0.5),
        "b_i": nrm(ks[12], (L, N_LRU_HEADS, LRU_HD), 0.02),
        "lru_lambda": lru_lambda,
        "gn_a": 1.0 + nrm(ks[14], (L, CONV_W), 0.02),
        "gn_b": 1.0 + nrm(ks[15], (L, LRU_W), 0.02),
        "w_out": nrm(ks[16], (L, MIX_WIDTH, D_MODEL), MIX_WIDTH ** -0.5),
        "norm2_g": 1.0 + nrm(ks[17], (L, D_MODEL), 0.02),
        "w_q": nrm(ks[18], (L, D_MODEL, PEER_HEADS * PEER_DK), D_MODEL ** -0.5),
        "sub_keys": nrm(ks[19], (L, PEER_HEADS, 2, N_KEYS, PEER_DK_HALF), PEER_DK_HALF ** -0.5),
        "expert_u": nrm(ks[20], (L, N_EXPERTS, D_MODEL), D_MODEL ** -0.5),
        "expert_v": nrm(ks[21], (L, N_EXPERTS, D_MODEL), PEER_HEADS ** -0.5),
        "final_g": 1.0 + nrm(ks[22], (D_MODEL,), 0.02),
    }


def reference(x, c, w_ada, b_ada, norm1_g, w_in, conv_a_w, conv_b_w, conv_b_b,
              w_r, b_r, w_i, b_i, lru_lambda, gn_a, gn_b, w_out,
              norm2_g, w_q, sub_keys, expert_u, expert_v, final_g):
    c_act = jax.nn.silu(c)
    for l in range(DEPTH):
        ada = c_act @ w_ada[l] + b_ada[l]
        sh1, sc1, g1, sh2, sc2, g2 = jnp.split(ada, 6, axis=-1)

        h = modulate(rmsnorm(x, norm1_g[l]), sh1, sc1)
        z = h @ w_in[l]
        gate_b, gate_c, xa, xr, gr = jnp.split(
            z, [CONV_W, 2 * CONV_W, 3 * CONV_W, 3 * CONV_W + LRU_W], axis=-1)

        y_a = gate_b * causal_dwconv(gate_c * xa, conv_a_w[l])

        xr = causal_dwconv(xr, conv_b_w[l]) + conv_b_b[l]
        y_b = rg_lru(xr, w_r[l], b_r[l], w_i[l], b_i[l], lru_lambda[l]) * jax.nn.gelu(gr)

        y = jnp.concatenate([head_rmsnorm(y_a, gn_a[l], N_CONV_HEADS),
                             head_rmsnorm(y_b, gn_b[l], N_LRU_HEADS)], axis=-1) @ w_out[l]
        x = x + g1[:, None, :] * y

        h = modulate(rmsnorm(x, norm2_g[l]), sh2, sc2)
        x = x + g2[:, None, :] * peer(h, w_q[l], sub_keys[l], expert_u[l], expert_v[l])

    return rmsnorm(x, final_g)
```

```python
import functools

import numpy as np
import jax
import jax.numpy as jnp
from jax import lax
from jax.experimental import pallas as pl
from jax.experimental.pallas import tpu as pltpu

EPS = 1e-6
LRU_C = 8.0
N_HEADS = 8
TOPK = 16
N_KEYS = 128
SHORT_K = 3
LRU_K = 4
HALO = 8

SUBLANES = 8
LANES = 128
VMEM_LIMIT_BYTES = 56 * 1024 * 1024

W_PITCH = 136

F32 = jnp.float32
BF16 = jnp.bfloat16
NT_DIMS = (((1,), (1,)), ((), ()))


def _dot(a, b):
    return jnp.dot(a, b, preferred_element_type=F32)


def _dot_nt(a, b, precision=None):
    return lax.dot_general(a, b, NT_DIMS, preferred_element_type=F32, precision=precision)


def _rms(x, g):
    return x * lax.rsqrt(jnp.mean(x * x, axis=-1, keepdims=True) + EPS) * g


def _adaln_kernel(c_ref, w_ref, b_ref, o_ref):
    c = c_ref[...]
    c_act = c * jax.nn.sigmoid(c)
    o_ref[...] = jnp.dot(c_act, w_ref[...], preferred_element_type=F32,
                         precision=lax.Precision.HIGHEST) + b_ref[...]


def _adaln(c, w, b):
    bsz, d = c.shape
    n = w.shape[1]
    blk = d
    return pl.pallas_call(
        _adaln_kernel,
        grid=(n // blk,),
        in_specs=[pl.BlockSpec((bsz, d), lambda j: (0, 0)),
                  pl.BlockSpec((d, blk), lambda j: (0, j)),
                  pl.BlockSpec((1, blk), lambda j: (0, j))],
        out_specs=pl.BlockSpec((bsz, blk), lambda j: (0, j)),
        out_shape=jax.ShapeDtypeStruct((bsz, n), F32),
        compiler_params=pltpu.CompilerParams(vmem_limit_bytes=VMEM_LIMIT_BYTES),
        name="adaln",
    )(c, w, b.reshape(1, n))


def _causal_conv(buf_ref, v, w_ref, s_idx, ts, k_w):
    @pl.when(s_idx == 0)
    def _():
        buf_ref[0:HALO, :] = jnp.zeros((HALO, v.shape[1]), F32)

    @pl.when(s_idx > 0)
    def _():
        buf_ref[0:HALO, :] = buf_ref[ts:ts + HALO, :]

    buf_ref[HALO:HALO + ts, :] = v
    out = v * w_ref[k_w - 1:k_w, :]
    for k in range(k_w - 1):
        shift = k_w - 1 - k
        out = out + buf_ref[HALO - shift:HALO - shift + ts, :] * w_ref[k:k + 1, :]
    return out


def _head_ms(y, m_ref):
    sq = y * y
    hi = sq.astype(BF16)
    lo = (sq - hi.astype(F32)).astype(BF16)
    return _dot(hi, m_ref[...]) + _dot(lo, m_ref[...])


def _mixer_kernel(x_ref, ada_ref, n1g_ref, win_ref, caw_ref, cbw_ref, cbb_ref, wg_ref, bg_ref,
                  lam_ref, gna_ref, gnb_ref, hm_ref, wout_ref, n2g_ref,
                  x1_ref, h2_ref, bufa_ref, bufb_ref, carry_ref, *, ts, cw):
    s_idx = pl.program_id(1)
    x = x_ref[0]
    ada = ada_ref[0]
    sh1, sc1, g1 = ada[0:1], ada[1:2], ada[2:3]
    sh2, sc2 = ada[3:4], ada[4:5]

    h = _rms(x, n1g_ref[...]) * (1.0 + sc1) + sh1
    z = _dot(h.astype(BF16), win_ref[...])
    gate_b = z[:, 0:cw]
    gate_c = z[:, cw:2 * cw]
    xa = z[:, 2 * cw:3 * cw]
    xr = z[:, 3 * cw:4 * cw]
    gr = z[:, 4 * cw:5 * cw]

    y_a = gate_b * _causal_conv(bufa_ref, gate_c * xa, caw_ref, s_idx, ts, SHORT_K)

    xc = _causal_conv(bufb_ref, xr, cbw_ref, s_idx, ts, LRU_K) + cbb_ref[...]
    pre = _dot(xc.astype(BF16), wg_ref[...]) + bg_ref[...]
    r = jax.nn.sigmoid(pre[:, 0:cw])
    i = jax.nn.sigmoid(pre[:, cw:2 * cw])
    nl = -lam_ref[...]
    softplus = jnp.maximum(nl, 0.0) + jnp.log1p(jnp.exp(-jnp.abs(nl)))
    log_a = (-LRU_C) * r * softplus
    a = jnp.exp(log_a)
    u = jnp.sqrt(-jnp.tanh(log_a) * (a * a + 1.0)) * (i * xc)

    row = lax.broadcasted_iota(jnp.int32, (ts, cw), 0)
    acc_a, acc_b = a, u
    d = 1
    while d < ts:
        keep = row >= d
        a_sh = jnp.where(keep, pltpu.roll(acc_a, d, 0), 1.0)
        b_sh = jnp.where(keep, pltpu.roll(acc_b, d, 0), 0.0)
        acc_b = acc_a * b_sh + acc_b
        acc_a = acc_a * a_sh
        d *= 2

    @pl.when(s_idx == 0)
    def _():
        carry_ref[...] = jnp.zeros_like(carry_ref)

    hseq = acc_a * carry_ref[0:1, :] + acc_b
    carry_ref[0:1, :] = hseq[ts - 1:ts, :]

    y_b = hseq * jax.nn.gelu(gr, approximate=True)

    na = y_a * lax.rsqrt(_head_ms(y_a, hm_ref) + EPS) * gna_ref[...]
    nb = y_b * lax.rsqrt(_head_ms(y_b, hm_ref) + EPS) * gnb_ref[...]
    y = _dot(na.astype(BF16), wout_ref[0:cw, :]) + _dot(nb.astype(BF16), wout_ref[cw:2 * cw, :])

    x1 = x + g1 * y
    x1_ref[0] = x1
    h2_ref[0] = (_rms(x1, n2g_ref[...]) * (1.0 + sc2) + sh2).astype(BF16)


def _mixer(x, ada, n1g, w_in, caw, cbw, cbb, w_gate, b_gate, lam, gna, gnb, head_mean, w_out, n2g):
    bsz, seq, d = x.shape
    cw = caw.shape[1]
    ts = min(512, seq)
    assert seq % ts == 0 and ts % SUBLANES == 0
    full = lambda a: pl.BlockSpec(a.shape, lambda b, s: (0,) * a.ndim)
    kern = functools.partial(_mixer_kernel, ts=ts, cw=cw)
    return pl.pallas_call(
        kern,
        grid=(bsz, seq // ts),
        in_specs=[pl.BlockSpec((1, ts, d), lambda b, s: (b, s, 0)),
                  pl.BlockSpec((1,) + ada.shape[1:], lambda b, s: (b, 0, 0)),
                  full(n1g), full(w_in), full(caw), full(cbw), full(cbb), full(w_gate),
                  full(b_gate), full(lam), full(gna), full(gnb), full(head_mean), full(w_out),
                  full(n2g)],
        out_specs=[pl.BlockSpec((1, ts, d), lambda b, s: (b, s, 0)),
                   pl.BlockSpec((1, ts, d), lambda b, s: (b, s, 0))],
        out_shape=[jax.ShapeDtypeStruct((bsz, seq, d), F32),
                   jax.ShapeDtypeStruct((bsz, seq, d), BF16)],
        scratch_shapes=[pltpu.VMEM((ts + HALO, cw), F32),
                        pltpu.VMEM((ts + HALO, cw), F32),
                        pltpu.VMEM((SUBLANES, cw), F32)],
        compiler_params=pltpu.CompilerParams(
            dimension_semantics=("arbitrary", "arbitrary"),
            vmem_limit_bytes=VMEM_LIMIT_BYTES),
        name="mixer",
    )(x, ada, n1g, w_in, caw, cbw, cbb, w_gate, b_gate, lam, gna, gnb, head_mean, w_out, n2g)


def _keyfold_kernel(k_ref, wqt_ref, o_ref):
    o_ref[0] = jnp.dot(k_ref[0], wqt_ref[0], preferred_element_type=F32,
                       precision=lax.Precision.HIGHEST).astype(BF16)


def _keyfold(keys, wq_t):
    g, n, dk = keys.shape
    d = wq_t.shape[2]
    return pl.pallas_call(
        _keyfold_kernel,
        grid=(g,),
        in_specs=[pl.BlockSpec((1, n, dk), lambda i: (i, 0, 0)),
                  pl.BlockSpec((1, dk, d), lambda i: (i, 0, 0))],
        out_specs=pl.BlockSpec((1, n, d), lambda i: (i, 0, 0)),
        out_shape=jax.ShapeDtypeStruct((g, n, d), BF16),
        compiler_params=pltpu.CompilerParams(vmem_limit_bytes=VMEM_LIMIT_BYTES),
        name="keyfold",
    )(keys, wq_t)


def _candidate_layout():
    groups = []
    codes = []
    ra = 0
    while ra < TOPK:
        nb = TOPK // (ra + 1)
        if nb > 1:
            rows = -(-nb // SUBLANES) * SUBLANES
            groups.append(("row", ra, rows))
            codes += [ra * TOPK + rb if rb < nb else -1 for rb in range(rows)]
            ra += 1
        else:
            n_ra = TOPK - ra
            rows = -(-n_ra // SUBLANES) * SUBLANES
            groups.append(("col", ra, rows))
            codes += [(ra + k) * TOPK if k < n_ra else -1 for k in range(rows)]
            ra = TOPK
    return groups, np.asarray(codes, np.int32)


_CAND_GROUPS, _CAND_CODES = _candidate_layout()
_BIG = 1 << 20


def _top16(s, code):
    n, tt = s.shape
    rank = lax.broadcasted_iota(jnp.int32, (TOPK, tt), 0)
    vals = jnp.zeros((TOPK, tt), F32)
    ids = jnp.zeros((TOPK, tt), jnp.int32)
    for r in range(TOPK):
        m = jnp.max(s, axis=0, keepdims=True)
        sel_code = jnp.min(jnp.where(s == m, code, _BIG), axis=0, keepdims=True)
        s = jnp.where(code == sel_code, -jnp.inf, s)
        vals = jnp.where(rank == r, m, vals)
        ids = jnp.where(rank == r, sel_code, ids)
    return vals, ids


def _pick_rank(table, rk):
    out = jnp.zeros_like(table)
    for r in range(TOPK):
        out = jnp.where(rk == r, table[r:r + 1, :], out)
    return out


def _route_kernel(h_ref, ws_ref, code_ref, a_ref, b_ref, g_ref, *, tt):
    st = _dot_nt(ws_ref[...], h_ref[...])
    key_iota = lax.broadcasted_iota(jnp.int32, (N_KEYS, tt), 0)
    code = code_ref[...]
    valid = code >= 0
    code = jnp.where(valid, code, _BIG - 1)
    a_all, b_all, g_all = [], [], []
    for hd in range(N_HEADS):
        base = hd * 2 * N_KEYS
        s1, i1 = _top16(st[base:base + N_KEYS, :], key_iota)
        s2, i2 = _top16(st[base + N_KEYS:base + 2 * N_KEYS, :], key_iota)
        parts = []
        for kind, ra, rows in _CAND_GROUPS:
            if kind == "row":
                parts.append(s1[ra:ra + 1, :] + s2[0:rows, :])
            else:
                parts.append(s1[ra:ra + rows, :] + s2[0:1, :])
        cand = jnp.where(valid, jnp.concatenate(parts, axis=0), -jnp.inf)
        best_s, best_c = _top16(cand, code)
        ra_sel = lax.shift_right_logical(best_c, 4)
        rb_sel = jnp.bitwise_and(best_c, TOPK - 1)
        e = jnp.exp(best_s - best_s[0:1, :])
        g_all.append(e / jnp.sum(e, axis=0, keepdims=True))
        a_all.append(_pick_rank(i1, ra_sel))
        b_all.append(_pick_rank(i2, rb_sel))
    a_ref[...] = jnp.concatenate(a_all, axis=0).T
    b_ref[...] = jnp.concatenate(b_all, axis=0).T
    g_ref[...] = jnp.concatenate(g_all, axis=0).T


def _route(h2, ws):
    t, d = h2.shape
    tt = min(256, t)
    assert t % tt == 0 and TOPK * TOPK <= _BIG
    codes = jnp.asarray(np.broadcast_to(_CAND_CODES[:, None], (_CAND_CODES.shape[0], tt)))
    hk = N_HEADS * TOPK
    out = jax.ShapeDtypeStruct((t, hk), jnp.int32)
    return pl.pallas_call(
        functools.partial(_route_kernel, tt=tt),
        grid=(t // tt,),
        in_specs=[pl.BlockSpec((tt, d), lambda i: (i, 0)),
                  pl.BlockSpec(ws.shape, lambda i: (0, 0)),
                  pl.BlockSpec(codes.shape, lambda i: (0, 0))],
        out_specs=[pl.BlockSpec((tt, hk), lambda i: (i, 0))] * 3,
        out_shape=[out, out, jax.ShapeDtypeStruct((t, hk), F32)],
        compiler_params=pltpu.CompilerParams(
            dimension_semantics=("arbitrary",), vmem_limit_bytes=VMEM_LIMIT_BYTES),
        name="route",
    )(h2, ws, codes)


def _experts_kernel(h_ref, a_ref, b_ref, g_ref, u_ref, v_ref, x1_ref, ada_ref, fg_ref,
                    o_ref, w3_ref, acc_ref, *, tt, ec, final_norm):
    j = pl.program_id(1)
    rows_per_step = ec // N_KEYS

    @pl.when(j == 0)
    def _():
        acc_ref[...] = jnp.zeros_like(acc_ref)
        sub = lax.broadcasted_iota(jnp.int32, (N_KEYS, a_ref.shape[1]), 0)

        def build(t, carry):
            arow = a_ref[pl.ds(t, 1), :]
            brow = b_ref[pl.ds(t, 1), :]
            grow = g_ref[pl.ds(t, 1), :]
            pt = jnp.where(sub == arow, grow, 0.0).astype(BF16)
            qt = jnp.where(sub == brow, 1.0, 0.0).astype(BF16)
            off = pl.multiple_of(t * W_PITCH, SUBLANES)
            w3_ref[pl.ds(off, N_KEYS), :] = _dot_nt(pt, qt)
            return carry

        lax.fori_loop(0, tt, build, 0, unroll=2)

    act = _dot_nt(h_ref[...], u_ref[...])
    gel = 0.5 * act * (1.0 + lax.erf(act * np.float32(1.0 / np.sqrt(2.0))))
    parts = []
    for k in range(rows_per_step):
        wk = w3_ref[pl.ds(j * rows_per_step + k, tt, stride=W_PITCH), :]
        parts.append((wk * gel[:, k * N_KEYS:(k + 1) * N_KEYS]).astype(BF16))
    coef = jnp.concatenate(parts, axis=1)
    acc_ref[...] += _dot(coef, v_ref[...])

    @pl.when(j == pl.num_programs(1) - 1)
    def _():
        g2 = ada_ref[0][5:6]
        x2 = x1_ref[...] + g2 * acc_ref[...]
        if final_norm:
            x2 = _rms(x2, fg_ref[...])
        o_ref[...] = x2


def _experts(h2, a, b, g, u, v, x1, ada, final_g, seq, final_norm):
    t, d = h2.shape
    ne = u.shape[0]
    hk = a.shape[1]
    tt = min(256, seq)
    ec = 1024
    assert t % tt == 0 and seq % tt == 0 and ne % ec == 0 and ne == N_KEYS * N_KEYS
    kern = functools.partial(_experts_kernel, tt=tt, ec=ec, final_norm=final_norm)
    tok = lambda w: pl.BlockSpec((tt, w), lambda i, j: (i, 0))
    return pl.pallas_call(
        kern,
        grid=(t // tt, ne // ec),
        in_specs=[tok(d), tok(hk), tok(hk), tok(hk),
                  pl.BlockSpec((ec, d), lambda i, j: (j, 0)),
                  pl.BlockSpec((ec, d), lambda i, j: (j, 0)),
                  tok(d),
                  pl.BlockSpec((1,) + ada.shape[1:], lambda i, j: ((i * tt) // seq, 0, 0)),
                  pl.BlockSpec(final_g.shape, lambda i, j: (0, 0))],
        out_specs=tok(d),
        out_shape=jax.ShapeDtypeStruct((t, d), F32),
        scratch_shapes=[pltpu.VMEM((tt * W_PITCH, N_KEYS), F32),
                        pltpu.VMEM((tt, d), F32)],
        compiler_params=pltpu.CompilerParams(
            dimension_semantics=("arbitrary", "arbitrary"),
            vmem_limit_bytes=VMEM_LIMIT_BYTES),
        name="experts",
    )(h2, a, b, g, u, v, x1, ada, final_g)


def _block_diag(w):
    nh, hd, _ = w.shape
    eye = jnp.eye(nh, dtype=w.dtype)
    return (eye[:, None, :, None] * w[:, :, None, :]).reshape(nh * hd, nh * hd)


def kernel(x, c, w_ada, b_ada, norm1_g, w_in, conv_a_w, conv_b_w, conv_b_b, w_r, b_r, w_i, b_i,
           lru_lambda, gn_a, gn_b, w_out, norm2_g, w_q, sub_keys, expert_u, expert_v, final_g):
    bsz, seq, d = x.shape
    depth = w_ada.shape[0]
    cw = conv_a_w.shape[2]
    hd = cw // N_HEADS
    head_mean = _block_diag(jnp.full((N_HEADS, hd, hd), 1.0 / hd, F32)).astype(BF16)
    row = lambda p: p.reshape(1, -1)
    for l in range(depth):
        ada = _adaln(c, w_ada[l], b_ada[l]).reshape(bsz, 6, d)
        w_gate = jnp.concatenate([_block_diag(w_r[l]), _block_diag(w_i[l])], axis=1).astype(BF16)
        b_gate = jnp.concatenate([row(b_r[l]), row(b_i[l])], axis=1)
        x1, h2 = _mixer(x, ada, row(norm1_g[l]), w_in[l].astype(BF16), conv_a_w[l], conv_b_w[l],
                        row(conv_b_b[l]), w_gate, b_gate, row(lru_lambda[l]), row(gn_a[l]),
                        row(gn_b[l]), head_mean, w_out[l].astype(BF16), row(norm2_g[l]))
        dk = sub_keys.shape[-1]
        keys = sub_keys[l].reshape(N_HEADS * 2, N_KEYS, dk)
        wq_t = w_q[l].T.reshape(N_HEADS * 2, dk, d)
        ws = _keyfold(keys, wq_t).reshape(N_HEADS * 2 * N_KEYS, d)
        h2f = h2.reshape(bsz * seq, d)
        a, b, g = _route(h2f, ws)
        x = _experts(h2f, a, b, g, expert_u[l].astype(BF16), expert_v[l].astype(BF16),
                     x1.reshape(bsz * seq, d), ada, row(final_g), seq,
                     final_norm=(l == depth - 1)).reshape(bsz, seq, d)
    return x
```

```python
import functools

import numpy as np
import jax
import jax.numpy as jnp
from jax import lax
from jax.experimental import pallas as pl
from jax.experimental.pallas import tpu as pltpu

EPS = 1e-6
LRU_C = 8.0
N_HEADS = 8
TOPK = 16
N_KEYS = 128
SHORT_K = 3
LRU_K = 4
HALO = 8

SUBLANES = 8
LANES = 128
VMEM_LIMIT_BYTES = 56 * 1024 * 1024

W_PITCH = N_KEYS // 2 + SUBLANES
BUILD_UNROLL = 32

F32 = jnp.float32
BF16 = jnp.bfloat16
NT_DIMS = (((1,), (1,)), ((), ()))


def _dot(a, b):
    return jnp.dot(a, b, preferred_element_type=F32)


def _dot_nt(a, b, precision=None):
    return lax.dot_general(a, b, NT_DIMS, preferred_element_type=F32, precision=precision)


def _rms(x, g):
    return x * lax.rsqrt(jnp.mean(x * x, axis=-1, keepdims=True) + EPS) * g


def _adaln_kernel(c_ref, w_ref, b_ref, o_ref):
    c = c_ref[...]
    c_act = c * jax.nn.sigmoid(c)
    o_ref[...] = jnp.dot(c_act, w_ref[...], preferred_element_type=F32,
                         precision=lax.Precision.HIGHEST) + b_ref[...]


def _adaln(c, w, b):
    bsz, d = c.shape
    n = w.shape[1]
    blk = d
    return pl.pallas_call(
        _adaln_kernel,
        grid=(n // blk,),
        in_specs=[pl.BlockSpec((bsz, d), lambda j: (0, 0)),
                  pl.BlockSpec((d, blk), lambda j: (0, j)),
                  pl.BlockSpec((1, blk), lambda j: (0, j))],
        out_specs=pl.BlockSpec((bsz, blk), lambda j: (0, j)),
        out_shape=jax.ShapeDtypeStruct((bsz, n), F32),
        compiler_params=pltpu.CompilerParams(vmem_limit_bytes=VMEM_LIMIT_BYTES),
        name="adaln",
    )(c, w, b.reshape(1, n))


def _causal_conv(buf_ref, v, w_ref, s_idx, ts, k_w):
    @pl.when(s_idx == 0)
    def _():
        buf_ref[0:HALO, :] = jnp.zeros((HALO, v.shape[1]), F32)

    @pl.when(s_idx > 0)
    def _():
        buf_ref[0:HALO, :] = buf_ref[ts:ts + HALO, :]

    buf_ref[HALO:HALO + ts, :] = v
    out = v * w_ref[k_w - 1:k_w, :]
    for k in range(k_w - 1):
        shift = k_w - 1 - k
        out = out + buf_ref[HALO - shift:HALO - shift + ts, :] * w_ref[k:k + 1, :]
    return out


def _head_ms(y, m_ref):
    sq = y * y
    hi = sq.astype(BF16)
    lo = (sq - hi.astype(F32)).astype(BF16)
    return _dot(hi, m_ref[...]) + _dot(lo, m_ref[...])


def _mixer_kernel(x_ref, ada_ref, n1g_ref, win_ref, caw_ref, cbw_ref, cbb_ref, wg_ref, bg_ref,
                  lam_ref, gna_ref, gnb_ref, hm_ref, wout_ref, n2g_ref,
                  x1_ref, h2_ref, bufa_ref, bufb_ref, carry_ref, *, ts, cw):
    s_idx = pl.program_id(1)
    x = x_ref[0]
    ada = ada_ref[0]
    sh1, sc1, g1 = ada[0:1], ada[1:2], ada[2:3]
    sh2, sc2 = ada[3:4], ada[4:5]

    h = _rms(x, n1g_ref[...]) * (1.0 + sc1) + sh1
    z = _dot(h.astype(BF16), win_ref[...])
    gate_b = z[:, 0:cw]
    gate_c = z[:, cw:2 * cw]
    xa = z[:, 2 * cw:3 * cw]
    xr = z[:, 3 * cw:4 * cw]
    gr = z[:, 4 * cw:5 * cw]

    y_a = gate_b * _causal_conv(bufa_ref, gate_c * xa, caw_ref, s_idx, ts, SHORT_K)

    xc = _causal_conv(bufb_ref, xr, cbw_ref, s_idx, ts, LRU_K) + cbb_ref[...]
    pre = _dot(xc.astype(BF16), wg_ref[...]) + bg_ref[...]
    r = jax.nn.sigmoid(pre[:, 0:cw])
    i = jax.nn.sigmoid(pre[:, cw:2 * cw])
    nl = -lam_ref[...]
    softplus = jnp.maximum(nl, 0.0) + jnp.log1p(jnp.exp(-jnp.abs(nl)))
    log_a = (-LRU_C) * r * softplus
    a = jnp.exp(log_a)
    u = jnp.sqrt(-jnp.tanh(log_a) * (a * a + 1.0)) * (i * xc)

    row = lax.broadcasted_iota(jnp.int32, (ts, cw), 0)
    acc_a, acc_b = a, u
    d = 1
    while d < ts:
        keep = row >= d
        a_sh = jnp.where(keep, pltpu.roll(acc_a, d, 0), 1.0)
        b_sh = jnp.where(keep, pltpu.roll(acc_b, d, 0), 0.0)
        acc_b = acc_a * b_sh + acc_b
        acc_a = acc_a * a_sh
        d *= 2

    @pl.when(s_idx == 0)
    def _():
        carry_ref[...] = jnp.zeros_like(carry_ref)

    hseq = acc_a * carry_ref[0:1, :] + acc_b
    carry_ref[0:1, :] = hseq[ts - 1:ts, :]

    y_b = hseq * jax.nn.gelu(gr, approximate=True)

    na = y_a * lax.rsqrt(_head_ms(y_a, hm_ref) + EPS) * gna_ref[...]
    nb = y_b * lax.rsqrt(_head_ms(y_b, hm_ref) + EPS) * gnb_ref[...]
    y = _dot(na.astype(BF16), wout_ref[0:cw, :]) + _dot(nb.astype(BF16), wout_ref[cw:2 * cw, :])

    x1 = x + g1 * y
    x1_ref[0] = x1
    h2_ref[0] = (_rms(x1, n2g_ref[...]) * (1.0 + sc2) + sh2).astype(BF16)


def _mixer(x, ada, n1g, w_in, caw, cbw, cbb, w_gate, b_gate, lam, gna, gnb, head_mean, w_out, n2g):
    bsz, seq, d = x.shape
    cw = caw.shape[1]
    ts = min(512, seq)
    assert seq % ts == 0 and ts % SUBLANES == 0
    full = lambda a: pl.BlockSpec(a.shape, lambda b, s: (0,) * a.ndim)
    kern = functools.partial(_mixer_kernel, ts=ts, cw=cw)
    return pl.pallas_call(
        kern,
        grid=(bsz, seq // ts),
        in_specs=[pl.BlockSpec((1, ts, d), lambda b, s: (b, s, 0)),
                  pl.BlockSpec((1,) + ada.shape[1:], lambda b, s: (b, 0, 0)),
                  full(n1g), full(w_in), full(caw), full(cbw), full(cbb), full(w_gate),
                  full(b_gate), full(lam), full(gna), full(gnb), full(head_mean), full(w_out),
                  full(n2g)],
        out_specs=[pl.BlockSpec((1, ts, d), lambda b, s: (b, s, 0)),
                   pl.BlockSpec((1, ts, d), lambda b, s: (b, s, 0))],
        out_shape=[jax.ShapeDtypeStruct((bsz, seq, d), F32),
                   jax.ShapeDtypeStruct((bsz, seq, d), BF16)],
        scratch_shapes=[pltpu.VMEM((ts + HALO, cw), F32),
                        pltpu.VMEM((ts + HALO, cw), F32),
                        pltpu.VMEM((SUBLANES, cw), F32)],
        compiler_params=pltpu.CompilerParams(
            dimension_semantics=("arbitrary", "arbitrary"),
            vmem_limit_bytes=VMEM_LIMIT_BYTES),
        name="mixer",
    )(x, ada, n1g, w_in, caw, cbw, cbb, w_gate, b_gate, lam, gna, gnb, head_mean, w_out, n2g)


def _keyfold_kernel(k_ref, wqt_ref, o_ref):
    o_ref[0] = jnp.dot(k_ref[0], wqt_ref[0], preferred_element_type=F32,
                       precision=lax.Precision.HIGHEST).astype(BF16)


def _keyfold(keys, wq_t):
    g, n, dk = keys.shape
    d = wq_t.shape[2]
    return pl.pallas_call(
        _keyfold_kernel,
        grid=(g,),
        in_specs=[pl.BlockSpec((1, n, dk), lambda i: (i, 0, 0)),
                  pl.BlockSpec((1, dk, d), lambda i: (i, 0, 0))],
        out_specs=pl.BlockSpec((1, n, d), lambda i: (i, 0, 0)),
        out_shape=jax.ShapeDtypeStruct((g, n, d), BF16),
        compiler_params=pltpu.CompilerParams(vmem_limit_bytes=VMEM_LIMIT_BYTES),
        name="keyfold",
    )(keys, wq_t)


def _candidate_layout():
    groups = []
    codes = []
    ra = 0
    while ra < TOPK:
        nb = TOPK // (ra + 1)
        if nb > 1:
            rows = -(-nb // SUBLANES) * SUBLANES
            groups.append(("row", ra, rows))
            codes += [ra * TOPK + rb if rb < nb else -1 for rb in range(rows)]
            ra += 1
        else:
            n_ra = TOPK - ra
            rows = -(-n_ra // SUBLANES) * SUBLANES
            groups.append(("col", ra, rows))
            codes += [(ra + k) * TOPK if k < n_ra else -1 for k in range(rows)]
            ra = TOPK
    return groups, np.asarray(codes, np.float32)


_CAND_GROUPS, _CAND_CODES = _candidate_layout()
_BIG = float(1 << 20)


def _top16(s, code):
    n, tt = s.shape
    rank = lax.broadcasted_iota(jnp.int32, (TOPK, tt), 0)
    vals = jnp.zeros((TOPK, tt), F32)
    ids = jnp.zeros((TOPK, tt), F32)
    for r in range(TOPK):
        m = jnp.max(s, axis=0, keepdims=True)
        sel_code = jnp.min(jnp.where(s == m, code, _BIG), axis=0, keepdims=True)
        s = jnp.where(code == sel_code, -jnp.inf, s)
        vals = jnp.where(rank == r, m, vals)
        ids = jnp.where(rank == r, sel_code, ids)
    return vals, ids


def _pick_rank(table, rk):
    out = jnp.zeros_like(table)
    for r in range(TOPK):
        out = jnp.where(rk == r, table[r:r + 1, :], out)
    return out


def _route_kernel(h_ref, ws_ref, code_ref, a_ref, b_ref, g_ref, *, tt):
    st = _dot_nt(ws_ref[...], h_ref[...])
    key_iota = lax.broadcasted_iota(jnp.int32, (N_KEYS, tt), 0).astype(F32)
    code = code_ref[...]
    valid = code >= 0.0
    code = jnp.where(valid, code, _BIG - 1.0)
    a_all, b_all, g_all = [], [], []
    for hd in range(N_HEADS):
        base = hd * 2 * N_KEYS
        s1, i1 = _top16(st[base:base + N_KEYS, :], key_iota)
        s2, i2 = _top16(st[base + N_KEYS:base + 2 * N_KEYS, :], key_iota)
        parts = []
        for kind, ra, rows in _CAND_GROUPS:
            if kind == "row":
                parts.append(s1[ra:ra + 1, :] + s2[0:rows, :])
            else:
                parts.append(s1[ra:ra + rows, :] + s2[0:1, :])
        cand = jnp.where(valid, jnp.concatenate(parts, axis=0), -jnp.inf)
        best_s, best_c = _top16(cand, code)
        best_c = best_c.astype(jnp.int32)
        ra_sel = lax.shift_right_logical(best_c, 4)
        rb_sel = jnp.bitwise_and(best_c, TOPK - 1)
        e = jnp.exp(best_s - best_s[0:1, :])
        g_all.append(e / jnp.sum(e, axis=0, keepdims=True))
        a_all.append(_pick_rank(i1, ra_sel))
        b_all.append(_pick_rank(i2, rb_sel))
    a_ref[...] = jnp.concatenate(a_all, axis=0).T.astype(jnp.int32)
    b_ref[...] = jnp.concatenate(b_all, axis=0).T.astype(jnp.int32)
    g_ref[...] = jnp.concatenate(g_all, axis=0).T


def _route(h2, ws):
    t, d = h2.shape
    tt = min(256, t)
    assert t % tt == 0 and TOPK * TOPK <= _BIG
    codes = jnp.asarray(np.broadcast_to(_CAND_CODES[:, None], (_CAND_CODES.shape[0], tt)))
    hk = N_HEADS * TOPK
    out = jax.ShapeDtypeStruct((t, hk), jnp.int32)
    return pl.pallas_call(
        functools.partial(_route_kernel, tt=tt),
        grid=(t // tt,),
        in_specs=[pl.BlockSpec((tt, d), lambda i: (i, 0)),
                  pl.BlockSpec(ws.shape, lambda i: (0, 0)),
                  pl.BlockSpec(codes.shape, lambda i: (0, 0))],
        out_specs=[pl.BlockSpec((tt, hk), lambda i: (i, 0))] * 3,
        out_shape=[out, out, jax.ShapeDtypeStruct((t, hk), F32)],
        compiler_params=pltpu.CompilerParams(
            dimension_semantics=("arbitrary",), vmem_limit_bytes=VMEM_LIMIT_BYTES),
        name="route",
    )(h2, ws, codes)


def _experts_kernel(h_ref, a_ref, b_ref, g_ref, u_ref, v_ref, x1_ref, ada_ref, fg_ref,
                    o_ref, w3_ref, acc_ref, *, tt, ec, final_norm):
    j = pl.program_id(1)
    rows_per_step = ec // N_KEYS

    @pl.when(j == 0)
    def _():
        acc_ref[...] = jnp.zeros_like(acc_ref)
        sub = lax.broadcasted_iota(jnp.int32, (N_KEYS, a_ref.shape[1]), 0)

        def build(t, carry):
            arow = a_ref[pl.ds(t, 1), :]
            brow = b_ref[pl.ds(t, 1), :]
            grow = 0.5 * g_ref[pl.ds(t, 1), :]
            pt = jnp.where(sub == arow, grow, 0.0).astype(BF16)
            qt = jnp.where(sub == brow, 1.0, 0.0).astype(BF16)
            wt = _dot_nt(pt, qt).astype(BF16)
            off = pl.multiple_of(t * W_PITCH, SUBLANES)
            w3_ref[pl.ds(off, N_KEYS // 2), :] = pltpu.bitcast(wt, jnp.uint32)
            return carry

        lax.fori_loop(0, tt, build, 0, unroll=BUILD_UNROLL)

    act = _dot_nt(h_ref[...], u_ref[...])
    gel = act * (1.0 + lax.erf(act * np.float32(1.0 / np.sqrt(2.0))))
    parts = []
    for r in range(rows_per_step // 2):
        word = w3_ref[pl.ds(j * (rows_per_step // 2) + r, tt, stride=W_PITCH), :]
        w_even = lax.bitcast_convert_type(lax.shift_left(word, jnp.uint32(16)), F32)
        w_odd = lax.bitcast_convert_type(jnp.bitwise_and(word, jnp.uint32(0xFFFF0000)), F32)
        k = 2 * r
        parts.append((w_even * gel[:, k * N_KEYS:(k + 1) * N_KEYS]).astype(BF16))
        parts.append((w_odd * gel[:, (k + 1) * N_KEYS:(k + 2) * N_KEYS]).astype(BF16))
    coef = jnp.concatenate(parts, axis=1)
    acc_ref[...] += _dot(coef, v_ref[...])

    @pl.when(j == pl.num_programs(1) - 1)
    def _():
        g2 = ada_ref[0][5:6]
        x2 = x1_ref[...] + g2 * acc_ref[...]
        if final_norm:
            x2 = _rms(x2, fg_ref[...])
        o_ref[...] = x2


def _experts(h2, a, b, g, u, v, x1, ada, final_g, seq, final_norm):
    t, d = h2.shape
    ne = u.shape[0]
    hk = a.shape[1]
    tt = min(512, seq)
    ec = 1024
    assert t % tt == 0 and seq % tt == 0 and ne % ec == 0 and ne == N_KEYS * N_KEYS
    assert ec % (2 * N_KEYS) == 0
    kern = functools.partial(_experts_kernel, tt=tt, ec=ec, final_norm=final_norm)
    tok = lambda w: pl.BlockSpec((tt, w), lambda i, j: (i, 0))
    return pl.pallas_call(
        kern,
        grid=(t // tt, ne // ec),
        in_specs=[tok(d), tok(hk), tok(hk), tok(hk),
                  pl.BlockSpec((ec, d), lambda i, j: (j, 0)),
                  pl.BlockSpec((ec, d), lambda i, j: (j, 0)),
                  tok(d),
                  pl.BlockSpec((1,) + ada.shape[1:], lambda i, j: ((i * tt) // seq, 0, 0)),
                  pl.BlockSpec(final_g.shape, lambda i, j: (0, 0))],
        out_specs=tok(d),
        out_shape=jax.ShapeDtypeStruct((t, d), F32),
        scratch_shapes=[pltpu.VMEM((tt * W_PITCH, N_KEYS), jnp.uint32),
                        pltpu.VMEM((tt, d), F32)],
        compiler_params=pltpu.CompilerParams(
            dimension_semantics=("arbitrary", "arbitrary"),
            vmem_limit_bytes=VMEM_LIMIT_BYTES),
        name="experts",
    )(h2, a, b, g, u, v, x1, ada, final_g)


def _block_diag(w):
    nh, hd, _ = w.shape
    eye = jnp.eye(nh, dtype=w.dtype)
    return (eye[:, None, :, None] * w[:, :, None, :]).reshape(nh * hd, nh * hd)


def kernel(x, c, w_ada, b_ada, norm1_g, w_in, conv_a_w, conv_b_w, conv_b_b, w_r, b_r, w_i, b_i,
           lru_lambda, gn_a, gn_b, w_out, norm2_g, w_q, sub_keys, expert_u, expert_v, final_g):
    bsz, seq, d = x.shape
    depth = w_ada.shape[0]
    cw = conv_a_w.shape[2]
    hd = cw // N_HEADS
    head_mean = _block_diag(jnp.full((N_HEADS, hd, hd), 1.0 / hd, F32)).astype(BF16)
    row = lambda p: p.reshape(1, -1)
    for l in range(depth):
        ada = _adaln(c, w_ada[l], b_ada[l]).reshape(bsz, 6, d)
        w_gate = jnp.concatenate([_block_diag(w_r[l]), _block_diag(w_i[l])], axis=1).astype(BF16)
        b_gate = jnp.concatenate([row(b_r[l]), row(b_i[l])], axis=1)
        x1, h2 = _mixer(x, ada, row(norm1_g[l]), w_in[l].astype(BF16), conv_a_w[l], conv_b_w[l],
                        row(conv_b_b[l]), w_gate, b_gate, row(lru_lambda[l]), row(gn_a[l]),
                        row(gn_b[l]), head_mean, w_out[l].astype(BF16), row(norm2_g[l]))
        dk = sub_keys.shape[-1]
        keys = sub_keys[l].reshape(N_HEADS * 2, N_KEYS, dk)
        wq_t = w_q[l].T.reshape(N_HEADS * 2, dk, d)
        ws = _keyfold(keys, wq_t).reshape(N_HEADS * 2 * N_KEYS, d)
        h2f = h2.reshape(bsz * seq, d)
        a, b, g = _route(h2f, ws)
        x = _experts(h2f, a, b, g, expert_u[l].astype(BF16), expert_v[l].astype(BF16),
                     x1.reshape(bsz * seq, d), ada, row(final_g), seq,
                     final_norm=(l == depth - 1)).reshape(bsz, seq, d)
    return x
```

```python
import functools

import numpy as np
import jax
import jax.numpy as jnp
from jax import lax
from jax.experimental import pallas as pl
from jax.experimental.pallas import tpu as pltpu

EPS = 1e-6
LRU_C = 8.0
N_HEADS = 8
TOPK = 16
N_KEYS = 128
SHORT_K = 3
LRU_K = 4
HALO = 8

SUBLANES = 8
LANES = 128
VMEM_LIMIT_BYTES = 56 * 1024 * 1024

W_PITCH = N_KEYS // 2 + SUBLANES
BUILD_UNROLL = 32
EXPERT_TOKENS = 512
ROUTE_LANES = 256

F32 = jnp.float32
BF16 = jnp.bfloat16
NT_DIMS = (((1,), (1,)), ((), ()))


def _dot(a, b):
    return jnp.dot(a, b, preferred_element_type=F32)


def _dot_nt(a, b, precision=None):
    return lax.dot_general(a, b, NT_DIMS, preferred_element_type=F32, precision=precision)


def _rms(x, g):
    return x * lax.rsqrt(jnp.mean(x * x, axis=-1, keepdims=True) + EPS) * g


def _adaln_kernel(c_ref, w_ref, b_ref, o_ref):
    c = c_ref[...]
    c_act = c * jax.nn.sigmoid(c)
    o_ref[...] = jnp.dot(c_act, w_ref[...], preferred_element_type=F32,
                         precision=lax.Precision.HIGHEST) + b_ref[...]


def _adaln(c, w, b):
    bsz, d = c.shape
    n = w.shape[1]
    blk = d
    return pl.pallas_call(
        _adaln_kernel,
        grid=(n // blk,),
        in_specs=[pl.BlockSpec((bsz, d), lambda j: (0, 0)),
                  pl.BlockSpec((d, blk), lambda j: (0, j)),
                  pl.BlockSpec((1, blk), lambda j: (0, j))],
        out_specs=pl.BlockSpec((bsz, blk), lambda j: (0, j)),
        out_shape=jax.ShapeDtypeStruct((bsz, n), F32),
        compiler_params=pltpu.CompilerParams(vmem_limit_bytes=VMEM_LIMIT_BYTES),
        name="adaln",
    )(c, w, b.reshape(1, n))


def _causal_conv(buf_ref, v, w_ref, s_idx, ts, k_w):
    @pl.when(s_idx == 0)
    def _():
        buf_ref[0:HALO, :] = jnp.zeros((HALO, v.shape[1]), F32)

    @pl.when(s_idx > 0)
    def _():
        buf_ref[0:HALO, :] = buf_ref[ts:ts + HALO, :]

    buf_ref[HALO:HALO + ts, :] = v
    out = v * w_ref[k_w - 1:k_w, :]
    for k in range(k_w - 1):
        shift = k_w - 1 - k
        out = out + buf_ref[HALO - shift:HALO - shift + ts, :] * w_ref[k:k + 1, :]
    return out


def _head_ms(y, m_ref):
    sq = y * y
    hi = sq.astype(BF16)
    lo = (sq - hi.astype(F32)).astype(BF16)
    return _dot(hi, m_ref[...]) + _dot(lo, m_ref[...])


def _mixer_kernel(x_ref, ada_ref, n1g_ref, win_ref, caw_ref, cbw_ref, cbb_ref, wg_ref, bg_ref,
                  lam_ref, gna_ref, gnb_ref, hm_ref, wout_ref, n2g_ref,
                  x1_ref, h2_ref, bufa_ref, bufb_ref, carry_ref, *, ts, cw):
    s_idx = pl.program_id(1)
    x = x_ref[0]
    ada = ada_ref[0]
    sh1, sc1, g1 = ada[0:1], ada[1:2], ada[2:3]
    sh2, sc2 = ada[3:4], ada[4:5]

    h = _rms(x, n1g_ref[...]) * (1.0 + sc1) + sh1
    z = _dot(h.astype(BF16), win_ref[...])
    gate_b = z[:, 0:cw]
    gate_c = z[:, cw:2 * cw]
    xa = z[:, 2 * cw:3 * cw]
    xr = z[:, 3 * cw:4 * cw]
    gr = z[:, 4 * cw:5 * cw]

    y_a = gate_b * _causal_conv(bufa_ref, gate_c * xa, caw_ref, s_idx, ts, SHORT_K)

    xc = _causal_conv(bufb_ref, xr, cbw_ref, s_idx, ts, LRU_K) + cbb_ref[...]
    pre = _dot(xc.astype(BF16), wg_ref[...]) + bg_ref[...]
    r = jax.nn.sigmoid(pre[:, 0:cw])
    i = jax.nn.sigmoid(pre[:, cw:2 * cw])
    nl = -lam_ref[...]
    softplus = jnp.maximum(nl, 0.0) + jnp.log1p(jnp.exp(-jnp.abs(nl)))
    log_a = (-LRU_C) * r * softplus
    a = jnp.exp(log_a)
    u = jnp.sqrt(-jnp.tanh(log_a) * (a * a + 1.0)) * (i * xc)

    row = lax.broadcasted_iota(jnp.int32, (ts, cw), 0)
    acc_a, acc_b = a, u
    d = 1
    while d < ts:
        keep = row >= d
        a_sh = jnp.where(keep, pltpu.roll(acc_a, d, 0), 1.0)
        b_sh = jnp.where(keep, pltpu.roll(acc_b, d, 0), 0.0)
        acc_b = acc_a * b_sh + acc_b
        acc_a = acc_a * a_sh
        d *= 2

    @pl.when(s_idx == 0)
    def _():
        carry_ref[...] = jnp.zeros_like(carry_ref)

    hseq = acc_a * carry_ref[0:1, :] + acc_b
    carry_ref[0:1, :] = hseq[ts - 1:ts, :]

    y_b = hseq * jax.nn.gelu(gr, approximate=True)

    na = y_a * lax.rsqrt(_head_ms(y_a, hm_ref) + EPS) * gna_ref[...]
    nb = y_b * lax.rsqrt(_head_ms(y_b, hm_ref) + EPS) * gnb_ref[...]
    y = _dot(na.astype(BF16), wout_ref[0:cw, :]) + _dot(nb.astype(BF16), wout_ref[cw:2 * cw, :])

    x1 = x + g1 * y
    x1_ref[0] = x1
    h2_ref[0] = (_rms(x1, n2g_ref[...]) * (1.0 + sc2) + sh2).astype(BF16)


def _mixer(x, ada, n1g, w_in, caw, cbw, cbb, w_gate, b_gate, lam, gna, gnb, head_mean, w_out, n2g):
    bsz, seq, d = x.shape
    cw = caw.shape[1]
    ts = min(512, seq)
    assert seq % ts == 0 and ts % SUBLANES == 0
    full = lambda a: pl.BlockSpec(a.shape, lambda b, s: (0,) * a.ndim)
    kern = functools.partial(_mixer_kernel, ts=ts, cw=cw)
    return pl.pallas_call(
        kern,
        grid=(bsz, seq // ts),
        in_specs=[pl.BlockSpec((1, ts, d), lambda b, s: (b, s, 0)),
                  pl.BlockSpec((1,) + ada.shape[1:], lambda b, s: (b, 0, 0)),
                  full(n1g), full(w_in), full(caw), full(cbw), full(cbb), full(w_gate),
                  full(b_gate), full(lam), full(gna), full(gnb), full(head_mean), full(w_out),
                  full(n2g)],
        out_specs=[pl.BlockSpec((1, ts, d), lambda b, s: (b, s, 0)),
                   pl.BlockSpec((1, ts, d), lambda b, s: (b, s, 0))],
        out_shape=[jax.ShapeDtypeStruct((bsz, seq, d), F32),
                   jax.ShapeDtypeStruct((bsz, seq, d), BF16)],
        scratch_shapes=[pltpu.VMEM((ts + HALO, cw), F32),
                        pltpu.VMEM((ts + HALO, cw), F32),
                        pltpu.VMEM((SUBLANES, cw), F32)],
        compiler_params=pltpu.CompilerParams(
            dimension_semantics=("arbitrary", "arbitrary"),
            vmem_limit_bytes=VMEM_LIMIT_BYTES),
        name="mixer",
    )(x, ada, n1g, w_in, caw, cbw, cbb, w_gate, b_gate, lam, gna, gnb, head_mean, w_out, n2g)


def _keyfold_kernel(k_ref, wqt_ref, o_ref):
    o_ref[0] = jnp.dot(k_ref[0], wqt_ref[0], preferred_element_type=F32,
                       precision=lax.Precision.HIGHEST).astype(BF16)


def _keyfold(keys, wq_t):
    g, n, dk = keys.shape
    d = wq_t.shape[2]
    return pl.pallas_call(
        _keyfold_kernel,
        grid=(g,),
        in_specs=[pl.BlockSpec((1, n, dk), lambda i: (i, 0, 0)),
                  pl.BlockSpec((1, dk, d), lambda i: (i, 0, 0))],
        out_specs=pl.BlockSpec((1, n, d), lambda i: (i, 0, 0)),
        out_shape=jax.ShapeDtypeStruct((g, n, d), BF16),
        compiler_params=pltpu.CompilerParams(vmem_limit_bytes=VMEM_LIMIT_BYTES),
        name="keyfold",
    )(keys, wq_t)


def _candidate_layout():
    groups = []
    ra = row0 = 0
    while ra < TOPK:
        nb = TOPK // (ra + 1)
        if nb > 1:
            kind, valid, step = "row", nb, 1
        else:
            kind, valid, step = "col", TOPK - ra, TOPK - ra
        rows = -(-valid // SUBLANES) * SUBLANES
        groups.append((kind, ra, valid, rows, row0))
        ra += step
        row0 += rows
    return groups


_CAND_GROUPS = _candidate_layout()
_BIG = float(1 << 20)


def _top16(s):
    nb, n = s.shape[0] // SUBLANES, s.shape[1]
    blocks = [s[SUBLANES * v:SUBLANES * (v + 1), :] for v in range(nb)]
    sub = lax.broadcasted_iota(jnp.int32, (SUBLANES, n), 0).astype(F32)
    vals, pos = [], []
    for _ in range(TOPK):
        best = blocks[0]
        first = jnp.zeros((SUBLANES, n), F32)
        for v in range(1, nb):
            first = jnp.where(blocks[v] > best, float(v), first)
            best = jnp.maximum(best, blocks[v])
        m = jnp.max(best, axis=0, keepdims=True)
        p = jnp.min(jnp.where(best == m, first * float(SUBLANES) + sub, _BIG), axis=0, keepdims=True)
        off = p - sub
        blocks = [jnp.where(off == float(SUBLANES * v), -jnp.inf, blocks[v]) for v in range(nb)]
        vals.append(m)
        pos.append(p)
    return jnp.concatenate(vals, axis=0), jnp.concatenate(pos, axis=0)


def _pick_rank(table, rk):
    out = jnp.zeros_like(table)
    for r in range(TOPK):
        out = jnp.where(rk == r, table[r:r + 1, :], out)
    return out


def _route_head(st):
    n = st.shape[1]
    s1, i1 = _top16(st[0:N_KEYS, :])
    s2, i2 = _top16(st[N_KEYS:2 * N_KEYS, :])
    sub = lax.broadcasted_iota(jnp.int32, (SUBLANES, n), 0)
    parts = []
    for kind, ra, valid, rows, _ in _CAND_GROUPS:
        if kind == "row":
            part = s1[ra:ra + 1, :] + s2[0:rows, :]
        else:
            part = s1[ra:ra + rows, :] + s2[0:1, :]
        if valid < rows:
            assert rows == SUBLANES
            part = jnp.where(sub < valid, part, -jnp.inf)
        parts.append(part)
    best_s, best_p = _top16(jnp.concatenate(parts, axis=0))
    ra_sel = jnp.zeros_like(best_p)
    rb_sel = jnp.zeros_like(best_p)
    for kind, ra, valid, rows, row0 in _CAND_GROUPS:
        in_group = best_p >= float(row0)
        local = best_p - float(row0)
        if kind == "row":
            ra_sel = jnp.where(in_group, float(ra), ra_sel)
            rb_sel = jnp.where(in_group, local, rb_sel)
        else:
            ra_sel = jnp.where(in_group, local + float(ra), ra_sel)
            rb_sel = jnp.where(in_group, 0.0, rb_sel)
    e = jnp.exp(best_s - best_s[0:1, :])
    g = e / jnp.sum(e, axis=0, keepdims=True)
    return _pick_rank(i1, ra_sel), _pick_rank(i2, rb_sel), g


def _route_kernel(h_ref, ws_ref, a_ref, b_ref, g_ref):
    st = _dot_nt(ws_ref[...], h_ref[...])
    a_all, b_all, g_all = [], [], []
    for hd in range(N_HEADS):
        a, b, g = _route_head(st[hd * 2 * N_KEYS:(hd + 1) * 2 * N_KEYS, :])
        a_all.append(a)
        b_all.append(b)
        g_all.append(g)
    a_ref[...] = jnp.concatenate(a_all, axis=0).T.astype(jnp.int32)
    b_ref[...] = jnp.concatenate(b_all, axis=0).T.astype(jnp.int32)
    g_ref[...] = jnp.concatenate(g_all, axis=0).T


def _route(h2, ws):
    t, d = h2.shape
    tt = min(ROUTE_LANES, t)
    assert t % tt == 0
    hk = N_HEADS * TOPK
    out = jax.ShapeDtypeStruct((t, hk), jnp.int32)
    return pl.pallas_call(
        _route_kernel,
        grid=(t // tt,),
        in_specs=[pl.BlockSpec((tt, d), lambda i: (i, 0)),
                  pl.BlockSpec(ws.shape, lambda i: (0, 0))],
        out_specs=[pl.BlockSpec((tt, hk), lambda i: (i, 0))] * 3,
        out_shape=[out, out, jax.ShapeDtypeStruct((t, hk), F32)],
        compiler_params=pltpu.CompilerParams(
            dimension_semantics=("arbitrary",), vmem_limit_bytes=VMEM_LIMIT_BYTES),
        name="route",
    )(h2, ws)


def _experts_kernel(h_ref, hn_ref, ws_ref, a0_ref, b0_ref, g0_ref, u_ref, v_ref,
                    x1_ref, ada_ref, fg_ref, o_ref,
                    w3_ref, acc_ref, a_ref, b_ref, g_ref, at_ref, bt_ref, gt_ref,
                    *, tt, ec, final_norm):
    i = pl.program_id(0)
    j = pl.program_id(1)
    n_steps = pl.num_programs(1)
    rows_per_step = ec // N_KEYS
    half = tt // 2

    @pl.when((i == 0) & (j == 0))
    def _():
        a_ref[...] = a0_ref[...]
        b_ref[...] = b0_ref[...]
        g_ref[...] = g0_ref[...]

    @pl.when(j == 0)
    def _():
        acc_ref[...] = jnp.zeros_like(acc_ref)
        sub = lax.broadcasted_iota(jnp.int32, (N_KEYS, a_ref.shape[1]), 0)

        def build(t, carry):
            arow = a_ref[pl.ds(t, 1), :]
            brow = b_ref[pl.ds(t, 1), :]
            grow = 0.5 * g_ref[pl.ds(t, 1), :]
            pt = jnp.where(sub == arow, grow, 0.0).astype(BF16)
            qt = jnp.where(sub == brow, 1.0, 0.0).astype(BF16)
            wt = _dot_nt(pt, qt).astype(BF16)
            off = pl.multiple_of(t * W_PITCH, SUBLANES)
            w3_ref[pl.ds(off, N_KEYS // 2), :] = pltpu.bitcast(wt, jnp.uint32)
            return carry

        lax.fori_loop(0, tt, build, 0, unroll=BUILD_UNROLL)

    tok_half = j // N_HEADS
    hn = hn_ref[pl.ds(pl.multiple_of(tok_half * half, half), half), :]
    scores = _dot_nt(ws_ref[...], hn)

    act = _dot_nt(h_ref[...], u_ref[...])
    gel = act * (1.0 + lax.erf(act * np.float32(1.0 / np.sqrt(2.0))))
    parts = []
    for r in range(rows_per_step // 2):
        word = w3_ref[pl.ds(j * (rows_per_step // 2) + r, tt, stride=W_PITCH), :]
        w_even = lax.bitcast_convert_type(lax.shift_left(word, jnp.uint32(16)), F32)
        w_odd = lax.bitcast_convert_type(jnp.bitwise_and(word, jnp.uint32(0xFFFF0000)), F32)
        k = 2 * r
        parts.append((w_even * gel[:, k * N_KEYS:(k + 1) * N_KEYS]).astype(BF16))
        parts.append((w_odd * gel[:, (k + 1) * N_KEYS:(k + 2) * N_KEYS]).astype(BF16))
    coef = jnp.concatenate(parts, axis=1)
    acc_ref[...] += _dot(coef, v_ref[...])

    ra, rb, rg = _route_head(scores)
    hd = j % N_HEADS
    at_ref[tok_half, hd] = ra
    bt_ref[tok_half, hd] = rb
    gt_ref[tok_half, hd] = rg

    @pl.when(j == n_steps - 1)
    def _():
        g2 = ada_ref[0][5:6]
        x2 = x1_ref[...] + g2 * acc_ref[...]
        if final_norm:
            x2 = _rms(x2, fg_ref[...])
        o_ref[...] = x2
        hk = N_HEADS * TOPK
        for hf in range(2):
            rows = slice(hf * half, (hf + 1) * half)
            a_ref[rows, :] = at_ref[hf].reshape(hk, half).T.astype(jnp.int32)
            b_ref[rows, :] = bt_ref[hf].reshape(hk, half).T.astype(jnp.int32)
            g_ref[rows, :] = gt_ref[hf].reshape(hk, half).T


def _experts(h2, ws, abg0, u, v, x1, ada, final_g, seq, final_norm):
    t, d = h2.shape
    ne = u.shape[0]
    hk = N_HEADS * TOPK
    tt = min(EXPERT_TOKENS, seq)
    ec = 1024
    half = tt // 2
    n_tiles = t // tt
    assert t % tt == 0 and seq % tt == 0 and ne == N_KEYS * N_KEYS
    assert ec % (2 * N_KEYS) == 0 and ne // ec == 2 * N_HEADS and tt % BUILD_UNROLL == 0
    kern = functools.partial(_experts_kernel, tt=tt, ec=ec, final_norm=final_norm)
    tok = lambda w: pl.BlockSpec((tt, w), lambda i, j: (i, 0))
    first = lambda w: pl.BlockSpec((tt, w), lambda i, j: (0, 0))
    scr_t = pltpu.VMEM((2, N_HEADS, TOPK, half), F32)
    return pl.pallas_call(
        kern,
        grid=(n_tiles, ne // ec),
        in_specs=[tok(d),
                  pl.BlockSpec((tt, d), lambda i, j: (jnp.minimum(i + 1, n_tiles - 1), 0)),
                  pl.BlockSpec((2 * N_KEYS, d), lambda i, j: (j % N_HEADS, 0)),
                  first(hk), first(hk), first(hk),
                  pl.BlockSpec((ec, d), lambda i, j: (j, 0)),
                  pl.BlockSpec((ec, d), lambda i, j: (j, 0)),
                  tok(d),
                  pl.BlockSpec((1,) + ada.shape[1:], lambda i, j: ((i * tt) // seq, 0, 0)),
                  pl.BlockSpec(final_g.shape, lambda i, j: (0, 0))],
        out_specs=tok(d),
        out_shape=jax.ShapeDtypeStruct((t, d), F32),
        scratch_shapes=[pltpu.VMEM((tt * W_PITCH, N_KEYS), jnp.uint32),
                        pltpu.VMEM((tt, d), F32),
                        pltpu.VMEM((tt, hk), jnp.int32),
                        pltpu.VMEM((tt, hk), jnp.int32),
                        pltpu.VMEM((tt, hk), F32),
                        scr_t, scr_t, scr_t],
        compiler_params=pltpu.CompilerParams(
            dimension_semantics=("arbitrary", "arbitrary"),
            vmem_limit_bytes=VMEM_LIMIT_BYTES),
        name="experts",
    )(h2, h2, ws, *abg0, u, v, x1, ada, final_g)


def _block_diag(w):
    nh, hd, _ = w.shape
    eye = jnp.eye(nh, dtype=w.dtype)
    return (eye[:, None, :, None] * w[:, :, None, :]).reshape(nh * hd, nh * hd)


def kernel(x, c, w_ada, b_ada, norm1_g, w_in, conv_a_w, conv_b_w, conv_b_b, w_r, b_r, w_i, b_i,
           lru_lambda, gn_a, gn_b, w_out, norm2_g, w_q, sub_keys, expert_u, expert_v, final_g):
    bsz, seq, d = x.shape
    depth = w_ada.shape[0]
    cw = conv_a_w.shape[2]
    hd = cw // N_HEADS
    head_mean = _block_diag(jnp.full((N_HEADS, hd, hd), 1.0 / hd, F32)).astype(BF16)
    row = lambda p: p.reshape(1, -1)
    for l in range(depth):
        ada = _adaln(c, w_ada[l], b_ada[l]).reshape(bsz, 6, d)
        w_gate = jnp.concatenate([_block_diag(w_r[l]), _block_diag(w_i[l])], axis=1).astype(BF16)
        b_gate = jnp.concatenate([row(b_r[l]), row(b_i[l])], axis=1)
        x1, h2 = _mixer(x, ada, row(norm1_g[l]), w_in[l].astype(BF16), conv_a_w[l], conv_b_w[l],
                        row(conv_b_b[l]), w_gate, b_gate, row(lru_lambda[l]), row(gn_a[l]),
                        row(gn_b[l]), head_mean, w_out[l].astype(BF16), row(norm2_g[l]))
        dk = sub_keys.shape[-1]
        keys = sub_keys[l].reshape(N_HEADS * 2, N_KEYS, dk)
        wq_t = w_q[l].T.reshape(N_HEADS * 2, dk, d)
        ws = _keyfold(keys, wq_t).reshape(N_HEADS * 2 * N_KEYS, d)
        h2f = h2.reshape(bsz * seq, d)
        abg0 = _route(h2f[:min(EXPERT_TOKENS, seq)], ws)
        x = _experts(h2f, ws, abg0, expert_u[l].astype(BF16), expert_v[l].astype(BF16),
                     x1.reshape(bsz * seq, d), ada, row(final_g), seq,
                     final_norm=(l == depth - 1)).reshape(bsz, seq, d)
    return x
```

```python
import functools

import numpy as np
import jax
import jax.numpy as jnp
from jax import lax
from jax.experimental import pallas as pl
from jax.experimental.pallas import tpu as pltpu

EPS = 1e-6
LRU_C = 8.0
N_HEADS = 8
TOPK = 16
N_KEYS = 128
SHORT_K = 3
LRU_K = 4
HALO = 8

SUBLANES = 8
LANES = 128
VMEM_LIMIT_BYTES = 56 * 1024 * 1024

W_PITCH = N_KEYS // 2 + SUBLANES
BUILD_UNROLL = 32
EXPERT_TOKENS = 512
ROUTE_LANES = 256

F32 = jnp.float32
BF16 = jnp.bfloat16
NT_DIMS = (((1,), (1,)), ((), ()))


def _dot(a, b):
    return jnp.dot(a, b, preferred_element_type=F32)


def _dot_nt(a, b, precision=None):
    return lax.dot_general(a, b, NT_DIMS, preferred_element_type=F32, precision=precision)


def _rms(x, g):
    return x * lax.rsqrt(jnp.mean(x * x, axis=-1, keepdims=True) + EPS) * g


def _adaln_kernel(c_ref, w_ref, b_ref, o_ref):
    c = c_ref[...]
    c_act = c * jax.nn.sigmoid(c)
    o_ref[...] = jnp.dot(c_act, w_ref[...], preferred_element_type=F32,
                         precision=lax.Precision.HIGHEST) + b_ref[...]


def _adaln(c, w, b):
    bsz, d = c.shape
    n = w.shape[1]
    blk = d
    return pl.pallas_call(
        _adaln_kernel,
        grid=(n // blk,),
        in_specs=[pl.BlockSpec((bsz, d), lambda j: (0, 0)),
                  pl.BlockSpec((d, blk), lambda j: (0, j)),
                  pl.BlockSpec((1, blk), lambda j: (0, j))],
        out_specs=pl.BlockSpec((bsz, blk), lambda j: (0, j)),
        out_shape=jax.ShapeDtypeStruct((bsz, n), F32),
        compiler_params=pltpu.CompilerParams(vmem_limit_bytes=VMEM_LIMIT_BYTES),
        name="adaln",
    )(c, w, b.reshape(1, n))


def _causal_conv(buf_ref, v, w_ref, s_idx, ts, k_w):
    @pl.when(s_idx == 0)
    def _():
        buf_ref[0:HALO, :] = jnp.zeros((HALO, v.shape[1]), F32)

    @pl.when(s_idx > 0)
    def _():
        buf_ref[0:HALO, :] = buf_ref[ts:ts + HALO, :]

    buf_ref[HALO:HALO + ts, :] = v
    out = v * w_ref[k_w - 1:k_w, :]
    for k in range(k_w - 1):
        shift = k_w - 1 - k
        out = out + buf_ref[HALO - shift:HALO - shift + ts, :] * w_ref[k:k + 1, :]
    return out


def _head_ms(y, m_ref):
    sq = y * y
    hi = sq.astype(BF16)
    lo = (sq - hi.astype(F32)).astype(BF16)
    return _dot(hi, m_ref[...]) + _dot(lo, m_ref[...])


def _mixer_kernel(x_ref, ada_ref, n1g_ref, win_ref, caw_ref, cbw_ref, cbb_ref, wg_ref, bg_ref,
                  lam_ref, gna_ref, gnb_ref, hm_ref, wout_ref, n2g_ref,
                  x1_ref, h2_ref, bufa_ref, bufb_ref, carry_ref, *, ts, cw):
    s_idx = pl.program_id(1)
    x = x_ref[0]
    ada = ada_ref[0]
    sh1, sc1, g1 = ada[0:1], ada[1:2], ada[2:3]
    sh2, sc2 = ada[3:4], ada[4:5]

    h = _rms(x, n1g_ref[...]) * (1.0 + sc1) + sh1
    z = _dot(h.astype(BF16), win_ref[...])
    gate_b = z[:, 0:cw]
    gate_c = z[:, cw:2 * cw]
    xa = z[:, 2 * cw:3 * cw]
    xr = z[:, 3 * cw:4 * cw]
    gr = z[:, 4 * cw:5 * cw]

    y_a = gate_b * _causal_conv(bufa_ref, gate_c * xa, caw_ref, s_idx, ts, SHORT_K)

    xc = _causal_conv(bufb_ref, xr, cbw_ref, s_idx, ts, LRU_K) + cbb_ref[...]
    pre = _dot(xc.astype(BF16), wg_ref[...]) + bg_ref[...]
    r = jax.nn.sigmoid(pre[:, 0:cw])
    i = jax.nn.sigmoid(pre[:, cw:2 * cw])
    nl = -lam_ref[...]
    softplus = jnp.maximum(nl, 0.0) + jnp.log1p(jnp.exp(-jnp.abs(nl)))
    log_a = (-LRU_C) * r * softplus
    a = jnp.exp(log_a)
    u = jnp.sqrt(-jnp.tanh(log_a) * (a * a + 1.0)) * (i * xc)

    row = lax.broadcasted_iota(jnp.int32, (ts, cw), 0)
    acc_a, acc_b = a, u
    d = 1
    while d < ts:
        keep = row >= d
        a_sh = jnp.where(keep, pltpu.roll(acc_a, d, 0), 1.0)
        b_sh = jnp.where(keep, pltpu.roll(acc_b, d, 0), 0.0)
        acc_b = acc_a * b_sh + acc_b
        acc_a = acc_a * a_sh
        d *= 2

    @pl.when(s_idx == 0)
    def _():
        carry_ref[...] = jnp.zeros_like(carry_ref)

    hseq = acc_a * carry_ref[0:1, :] + acc_b
    carry_ref[0:1, :] = hseq[ts - 1:ts, :]

    y_b = hseq * jax.nn.gelu(gr, approximate=True)

    na = y_a * lax.rsqrt(_head_ms(y_a, hm_ref) + EPS) * gna_ref[...]
    nb = y_b * lax.rsqrt(_head_ms(y_b, hm_ref) + EPS) * gnb_ref[...]
    y = _dot(na.astype(BF16), wout_ref[0:cw, :]) + _dot(nb.astype(BF16), wout_ref[cw:2 * cw, :])

    x1 = x + g1 * y
    x1_ref[0] = x1
    h2_ref[0] = (_rms(x1, n2g_ref[...]) * (1.0 + sc2) + sh2).astype(BF16)


def _mixer(x, ada, n1g, w_in, caw, cbw, cbb, w_gate, b_gate, lam, gna, gnb, head_mean, w_out, n2g):
    bsz, seq, d = x.shape
    cw = caw.shape[1]
    ts = min(512, seq)
    assert seq % ts == 0 and ts % SUBLANES == 0
    full = lambda a: pl.BlockSpec(a.shape, lambda b, s: (0,) * a.ndim)
    kern = functools.partial(_mixer_kernel, ts=ts, cw=cw)
    return pl.pallas_call(
        kern,
        grid=(bsz, seq // ts),
        in_specs=[pl.BlockSpec((1, ts, d), lambda b, s: (b, s, 0)),
                  pl.BlockSpec((1,) + ada.shape[1:], lambda b, s: (b, 0, 0)),
                  full(n1g), full(w_in), full(caw), full(cbw), full(cbb), full(w_gate),
                  full(b_gate), full(lam), full(gna), full(gnb), full(head_mean), full(w_out),
                  full(n2g)],
        out_specs=[pl.BlockSpec((1, ts, d), lambda b, s: (b, s, 0)),
                   pl.BlockSpec((1, ts, d), lambda b, s: (b, s, 0))],
        out_shape=[jax.ShapeDtypeStruct((bsz, seq, d), F32),
                   jax.ShapeDtypeStruct((bsz, seq, d), BF16)],
        scratch_shapes=[pltpu.VMEM((ts + HALO, cw), F32),
                        pltpu.VMEM((ts + HALO, cw), F32),
                        pltpu.VMEM((SUBLANES, cw), F32)],
        compiler_params=pltpu.CompilerParams(
            dimension_semantics=("arbitrary", "arbitrary"),
            vmem_limit_bytes=VMEM_LIMIT_BYTES),
        name="mixer",
    )(x, ada, n1g, w_in, caw, cbw, cbb, w_gate, b_gate, lam, gna, gnb, head_mean, w_out, n2g)


def _keyfold_kernel(k_ref, wqt_ref, o_ref):
    o_ref[0] = jnp.dot(k_ref[0], wqt_ref[0], preferred_element_type=F32,
                       precision=lax.Precision.HIGHEST).astype(BF16)


def _keyfold(keys, wq_t):
    g, n, dk = keys.shape
    d = wq_t.shape[2]
    return pl.pallas_call(
        _keyfold_kernel,
        grid=(g,),
        in_specs=[pl.BlockSpec((1, n, dk), lambda i: (i, 0, 0)),
                  pl.BlockSpec((1, dk, d), lambda i: (i, 0, 0))],
        out_specs=pl.BlockSpec((1, n, d), lambda i: (i, 0, 0)),
        out_shape=jax.ShapeDtypeStruct((g, n, d), BF16),
        compiler_params=pltpu.CompilerParams(vmem_limit_bytes=VMEM_LIMIT_BYTES),
        name="keyfold",
    )(keys, wq_t)


def _candidate_layout():
    groups = []
    ra = row0 = 0
    while ra < TOPK:
        nb = TOPK // (ra + 1)
        if nb > 1:
            kind, valid, step = "row", nb, 1
        else:
            kind, valid, step = "col", TOPK - ra, TOPK - ra
        rows = -(-valid // SUBLANES) * SUBLANES
        groups.append((kind, ra, valid, rows, row0))
        ra += step
        row0 += rows
    return groups


_CAND_GROUPS = _candidate_layout()
_BIG = float(1 << 20)


def _top16(s):
    nb, n = s.shape[0] // SUBLANES, s.shape[1]
    blocks = [s[SUBLANES * v:SUBLANES * (v + 1), :] for v in range(nb)]
    sub = lax.broadcasted_iota(jnp.int32, (SUBLANES, n), 0).astype(F32)
    vals, pos = [], []
    for _ in range(TOPK):
        best = blocks[0]
        first = jnp.zeros((SUBLANES, n), F32)
        for v in range(1, nb):
            first = jnp.where(blocks[v] > best, float(v), first)
            best = jnp.maximum(best, blocks[v])
        m = jnp.max(best, axis=0, keepdims=True)
        p = jnp.min(jnp.where(best == m, first * float(SUBLANES) + sub, _BIG), axis=0, keepdims=True)
        off = p - sub
        blocks = [jnp.where(off == float(SUBLANES * v), -jnp.inf, blocks[v]) for v in range(nb)]
        vals.append(m)
        pos.append(p)
    return jnp.concatenate(vals, axis=0), jnp.concatenate(pos, axis=0)


def _pick_rank(table, rk):
    out = jnp.zeros_like(table)
    for r in range(TOPK):
        out = jnp.where(rk == r, table[r:r + 1, :], out)
    return out


def _batcher_pairs(n):
    pairs = []
    p = 1
    while p < n:
        k = p
        while k >= 1:
            for j in range(k % p, n - k, 2 * k):
                for i in range(min(k, n - j - k)):
                    if (i + j) // (2 * p) == (i + j + k) // (2 * p):
                        pairs.append((i + j, i + j + k))
            k //= 2
        p *= 2
    return pairs


_SORT_PAIRS = _batcher_pairs(N_KEYS // SUBLANES)


def _merge_top(groups, n_out):
    out_v, out_i = [], []
    for r in range(n_out):
        heads = [(g[0][0], g[1][0]) for g in groups]
        top = heads[0][0]
        for hv, _ in heads[1:]:
            top = jnp.maximum(top, hv)
        m = jnp.max(top, axis=0, keepdims=True)
        cand = None
        for hv, hi in heads:
            c = jnp.where(hv == m, hi, _BIG)
            cand = c if cand is None else jnp.minimum(cand, c)
        p = jnp.min(cand, axis=0, keepdims=True)
        out_v.append(m)
        out_i.append(p)
        left = n_out - 1 - r
        if left == 0:
            break
        for vals, ids in groups:
            hit = ids[0] == p
            depth = len(vals)
            for d in range(min(depth - 1, left)):
                vals[d] = jnp.where(hit, vals[d + 1], vals[d])
                ids[d] = jnp.where(hit, ids[d + 1], ids[d])
            if depth - 1 < left:
                vals[depth - 1] = jnp.where(hit, -jnp.inf, vals[depth - 1])
    return jnp.concatenate(out_v, axis=0), jnp.concatenate(out_i, axis=0), out_v


def _top16_sorted(s):
    nb, n = s.shape[0] // SUBLANES, s.shape[1]
    assert nb == N_KEYS // SUBLANES
    sub = lax.broadcasted_iota(jnp.int32, (SUBLANES, n), 0).astype(F32)
    vals = [s[SUBLANES * v:SUBLANES * (v + 1), :] for v in range(nb)]
    ids = [sub + float(SUBLANES * v) for v in range(nb)]
    for i, j in _SORT_PAIRS:
        swap = vals[j] > vals[i]
        vals[i], vals[j] = jnp.maximum(vals[i], vals[j]), jnp.minimum(vals[i], vals[j])
        ids[i], ids[j] = jnp.where(swap, ids[j], ids[i]), jnp.where(swap, ids[i], ids[j])
    top_v, top_i, rows = _merge_top([[vals, ids]], TOPK + 1)
    ok = top_v[0:TOPK, :] > jnp.concatenate(rows[1:], axis=0)
    return top_v[0:TOPK, :], top_i[0:TOPK, :], ok


def _route_head_fast(st):
    n = st.shape[1]
    s1, i1, ok1 = _top16_sorted(st[0:N_KEYS, :])
    s2, i2, ok2 = _top16_sorted(st[N_KEYS:2 * N_KEYS, :])
    sub = lax.broadcasted_iota(jnp.int32, (SUBLANES, n), 0)
    sub_f = sub.astype(F32)
    head = s1[0:1, :] + s2
    lo_v, lo_i = [head[0:SUBLANES, :]], [sub_f]
    for ra in range(1, TOPK):
        nb = TOPK // (ra + 1)
        v = s1[ra:ra + 1, :] + s2[0:SUBLANES, :]
        c = sub_f + float(ra * TOPK)
        if nb < SUBLANES:
            v = jnp.where(sub < nb, v, -jnp.inf)
            c = jnp.where(sub < nb, c, _BIG)
        lo_v.append(v)
        lo_i.append(c)
    hi = [[head[SUBLANES:TOPK, :]], [sub_f + float(SUBLANES)]]
    best_s, best_c, _ = _merge_top([[lo_v, lo_i], hi], TOPK)
    best_c = best_c.astype(jnp.int32)
    ra_sel = lax.shift_right_logical(best_c, 4).astype(F32)
    rb_sel = jnp.bitwise_and(best_c, TOPK - 1).astype(F32)
    e = jnp.exp(best_s - best_s[0:1, :])
    g = e / jnp.sum(e, axis=0, keepdims=True)
    ok = jnp.where(ok1, 0.0, 1.0) + jnp.where(ok2, 0.0, 1.0)
    return _pick_rank(i1, ra_sel), _pick_rank(i2, rb_sel), g, jnp.max(ok) == 0.0


def _route_head(st):
    n = st.shape[1]
    s1, i1 = _top16(st[0:N_KEYS, :])
    s2, i2 = _top16(st[N_KEYS:2 * N_KEYS, :])
    sub = lax.broadcasted_iota(jnp.int32, (SUBLANES, n), 0)
    parts = []
    for kind, ra, valid, rows, _ in _CAND_GROUPS:
        if kind == "row":
            part = s1[ra:ra + 1, :] + s2[0:rows, :]
        else:
            part = s1[ra:ra + rows, :] + s2[0:1, :]
        if valid < rows:
            assert rows == SUBLANES
            part = jnp.where(sub < valid, part, -jnp.inf)
        parts.append(part)
    best_s, best_p = _top16(jnp.concatenate(parts, axis=0))
    ra_sel = jnp.zeros_like(best_p)
    rb_sel = jnp.zeros_like(best_p)
    for kind, ra, valid, rows, row0 in _CAND_GROUPS:
        in_group = best_p >= float(row0)
        local = best_p - float(row0)
        if kind == "row":
            ra_sel = jnp.where(in_group, float(ra), ra_sel)
            rb_sel = jnp.where(in_group, local, rb_sel)
        else:
            ra_sel = jnp.where(in_group, local + float(ra), ra_sel)
            rb_sel = jnp.where(in_group, 0.0, rb_sel)
    e = jnp.exp(best_s - best_s[0:1, :])
    g = e / jnp.sum(e, axis=0, keepdims=True)
    return _pick_rank(i1, ra_sel), _pick_rank(i2, rb_sel), g


def _route_kernel(h_ref, ws_ref, a_ref, b_ref, g_ref):
    st = _dot_nt(ws_ref[...], h_ref[...])
    a_all, b_all, g_all = [], [], []
    for hd in range(N_HEADS):
        a, b, g = _route_head(st[hd * 2 * N_KEYS:(hd + 1) * 2 * N_KEYS, :])
        a_all.append(a)
        b_all.append(b)
        g_all.append(g)
    a_ref[...] = jnp.concatenate(a_all, axis=0).T.astype(jnp.int32)
    b_ref[...] = jnp.concatenate(b_all, axis=0).T.astype(jnp.int32)
    g_ref[...] = jnp.concatenate(g_all, axis=0).T


def _route(h2, ws):
    t, d = h2.shape
    tt = min(ROUTE_LANES, t)
    assert t % tt == 0
    hk = N_HEADS * TOPK
    out = jax.ShapeDtypeStruct((t, hk), jnp.int32)
    return pl.pallas_call(
        _route_kernel,
        grid=(t // tt,),
        in_specs=[pl.BlockSpec((tt, d), lambda i: (i, 0)),
                  pl.BlockSpec(ws.shape, lambda i: (0, 0))],
        out_specs=[pl.BlockSpec((tt, hk), lambda i: (i, 0))] * 3,
        out_shape=[out, out, jax.ShapeDtypeStruct((t, hk), F32)],
        compiler_params=pltpu.CompilerParams(
            dimension_semantics=("arbitrary",), vmem_limit_bytes=VMEM_LIMIT_BYTES),
        name="route",
    )(h2, ws)


def _experts_kernel(h_ref, hn_ref, ws_ref, a0_ref, b0_ref, g0_ref, u_ref, v_ref,
                    x1_ref, ada_ref, fg_ref, o_ref,
                    w3_ref, acc_ref, coef_ref, a_ref, b_ref, g_ref, at_ref, bt_ref, gt_ref,
                    *, tt, ec, n_chunks, n_tiles, final_norm):
    s = pl.program_id(0)
    i = s // n_chunks
    j = s % n_chunks
    rows_per_step = ec // N_KEYS
    half = tt // 2

    @pl.when(s == 0)
    def _():
        a_ref[...] = a0_ref[...]
        b_ref[...] = b0_ref[...]
        g_ref[...] = g0_ref[...]
        acc_ref[...] = jnp.zeros_like(acc_ref)
        coef_ref[...] = jnp.zeros_like(coef_ref)

    @pl.when((j == 0) & (i < n_tiles))
    def _():
        sub = lax.broadcasted_iota(jnp.int32, (N_KEYS, a_ref.shape[1]), 0)

        def build(t, carry):
            arow = a_ref[pl.ds(t, 1), :]
            brow = b_ref[pl.ds(t, 1), :]
            grow = 0.5 * g_ref[pl.ds(t, 1), :]
            pt = jnp.where(sub == arow, grow, 0.0).astype(BF16)
            qt = jnp.where(sub == brow, 1.0, 0.0).astype(BF16)
            wt = _dot_nt(pt, qt).astype(BF16)
            off = pl.multiple_of(t * W_PITCH, SUBLANES)
            w3_ref[pl.ds(off, N_KEYS // 2), :] = pltpu.bitcast(wt, jnp.uint32)
            return carry

        lax.fori_loop(0, tt, build, 0, unroll=BUILD_UNROLL)

    tok_half = j // N_HEADS
    hn = hn_ref[pl.ds(pl.multiple_of(tok_half * half, half), half), :]
    scores = _dot_nt(ws_ref[...], hn)

    acc_ref[...] += _dot(coef_ref[(s + 1) % 2], v_ref[...])

    act = _dot_nt(h_ref[...], u_ref[...])
    gel = act * (1.0 + lax.erf(act * np.float32(1.0 / np.sqrt(2.0))))
    parts = []
    for r in range(rows_per_step // 2):
        word = w3_ref[pl.ds(j * (rows_per_step // 2) + r, tt, stride=W_PITCH), :]
        w_even = lax.bitcast_convert_type(lax.shift_left(word, jnp.uint32(16)), F32)
        w_odd = lax.bitcast_convert_type(jnp.bitwise_and(word, jnp.uint32(0xFFFF0000)), F32)
        k = 2 * r
        parts.append((w_even * gel[:, k * N_KEYS:(k + 1) * N_KEYS]).astype(BF16))
        parts.append((w_odd * gel[:, (k + 1) * N_KEYS:(k + 2) * N_KEYS]).astype(BF16))
    coef_ref[s % 2] = jnp.concatenate(parts, axis=1)

    ra, rb, rg, fast_ok = _route_head_fast(scores)
    hd = j % N_HEADS
    at_ref[tok_half, hd] = ra
    bt_ref[tok_half, hd] = rb
    gt_ref[tok_half, hd] = rg

    @pl.when(jnp.logical_not(fast_ok))
    def _():
        ra, rb, rg = _route_head(_dot_nt(ws_ref[...], hn_ref[pl.ds(
            pl.multiple_of(tok_half * half, half), half), :]))
        at_ref[tok_half, hd] = ra
        bt_ref[tok_half, hd] = rb
        gt_ref[tok_half, hd] = rg

    @pl.when((j == 0) & (s > 0))
    def _():
        g2 = ada_ref[0][5:6]
        x2 = x1_ref[...] + g2 * acc_ref[...]
        if final_norm:
            x2 = _rms(x2, fg_ref[...])
        o_ref[...] = x2
        acc_ref[...] = jnp.zeros_like(acc_ref)

    @pl.when(j == n_chunks - 1)
    def _():
        hk = N_HEADS * TOPK
        for hf in range(2):
            rows = slice(hf * half, (hf + 1) * half)
            a_ref[rows, :] = at_ref[hf].reshape(hk, half).T.astype(jnp.int32)
            b_ref[rows, :] = bt_ref[hf].reshape(hk, half).T.astype(jnp.int32)
            g_ref[rows, :] = gt_ref[hf].reshape(hk, half).T


def _experts(h2, ws, abg0, u, v, x1, ada, final_g, seq, final_norm):
    t, d = h2.shape
    ne = u.shape[0]
    hk = N_HEADS * TOPK
    tt = min(EXPERT_TOKENS, seq)
    ec = 1024
    half = tt // 2
    n_tiles = t // tt
    assert t % tt == 0 and seq % tt == 0 and ne == N_KEYS * N_KEYS
    assert ec % (2 * N_KEYS) == 0 and ne // ec == 2 * N_HEADS and tt % BUILD_UNROLL == 0
    nc = ne // ec
    kern = functools.partial(_experts_kernel, tt=tt, ec=ec, n_chunks=nc, n_tiles=n_tiles,
                             final_norm=final_norm)
    tile = lambda s: jnp.minimum(s // nc, n_tiles - 1)
    done = lambda s: jnp.maximum(s - 1, 0) // nc
    first = lambda w: pl.BlockSpec((tt, w), lambda s: (0, 0))
    scr_t = pltpu.VMEM((2, N_HEADS, TOPK, half), F32)
    return pl.pallas_call(
        kern,
        grid=(n_tiles * nc + 1,),
        in_specs=[pl.BlockSpec((tt, d), lambda s: (tile(s), 0)),
                  pl.BlockSpec((tt, d), lambda s: (jnp.minimum(s // nc + 1, n_tiles - 1), 0)),
                  pl.BlockSpec((2 * N_KEYS, d), lambda s: (s % N_HEADS, 0)),
                  first(hk), first(hk), first(hk),
                  pl.BlockSpec((ec, d), lambda s: (s % nc, 0)),
                  pl.BlockSpec((ec, d), lambda s: ((s + nc - 1) % nc, 0)),
                  pl.BlockSpec((tt, d), lambda s: (done(s), 0)),
                  pl.BlockSpec((1,) + ada.shape[1:], lambda s: ((done(s) * tt) // seq, 0, 0)),
                  pl.BlockSpec(final_g.shape, lambda s: (0, 0))],
        out_specs=pl.BlockSpec((tt, d), lambda s: (done(s), 0)),
        out_shape=jax.ShapeDtypeStruct((t, d), F32),
        scratch_shapes=[pltpu.VMEM((tt * W_PITCH, N_KEYS), jnp.uint32),
                        pltpu.VMEM((tt, d), F32),
                        pltpu.VMEM((2, tt, ec), BF16),
                        pltpu.VMEM((tt, hk), jnp.int32),
                        pltpu.VMEM((tt, hk), jnp.int32),
                        pltpu.VMEM((tt, hk), F32),
                        scr_t, scr_t, scr_t],
        compiler_params=pltpu.CompilerParams(
            dimension_semantics=("arbitrary",),
            vmem_limit_bytes=VMEM_LIMIT_BYTES),
        name="experts",
    )(h2, h2, ws, *abg0, u, v, x1, ada, final_g)


def _block_diag(w):
    nh, hd, _ = w.shape
    eye = jnp.eye(nh, dtype=w.dtype)
    return (eye[:, None, :, None] * w[:, :, None, :]).reshape(nh * hd, nh * hd)


def kernel(x, c, w_ada, b_ada, norm1_g, w_in, conv_a_w, conv_b_w, conv_b_b, w_r, b_r, w_i, b_i,
           lru_lambda, gn_a, gn_b, w_out, norm2_g, w_q, sub_keys, expert_u, expert_v, final_g):
    bsz, seq, d = x.shape
    depth = w_ada.shape[0]
    cw = conv_a_w.shape[2]
    hd = cw // N_HEADS
    head_mean = _block_diag(jnp.full((N_HEADS, hd, hd), 1.0 / hd, F32)).astype(BF16)
    row = lambda p: p.reshape(1, -1)
    for l in range(depth):
        ada = _adaln(c, w_ada[l], b_ada[l]).reshape(bsz, 6, d)
        w_gate = jnp.concatenate([_block_diag(w_r[l]), _block_diag(w_i[l])], axis=1).astype(BF16)
        b_gate = jnp.concatenate([row(b_r[l]), row(b_i[l])], axis=1)
        x1, h2 = _mixer(x, ada, row(norm1_g[l]), w_in[l].astype(BF16), conv_a_w[l], conv_b_w[l],
                        row(conv_b_b[l]), w_gate, b_gate, row(lru_lambda[l]), row(gn_a[l]),
                        row(gn_b[l]), head_mean, w_out[l].astype(BF16), row(norm2_g[l]))
        dk = sub_keys.shape[-1]
        keys = sub_keys[l].reshape(N_HEADS * 2, N_KEYS, dk)
        wq_t = w_q[l].T.reshape(N_HEADS * 2, dk, d)
        ws = _keyfold(keys, wq_t).reshape(N_HEADS * 2 * N_KEYS, d)
        h2f = h2.reshape(bsz * seq, d)
        abg0 = _route(h2f[:min(EXPERT_TOKENS, seq)], ws)
        x = _experts(h2f, ws, abg0, expert_u[l].astype(BF16), expert_v[l].astype(BF16),
                     x1.reshape(bsz * seq, d), ada, row(final_g), seq,
                     final_norm=(l == depth - 1)).reshape(bsz, seq, d)
    return x
```

```python
import functools

import numpy as np
import jax
import jax.numpy as jnp
from jax import lax
from jax.experimental import pallas as pl
from jax.experimental.pallas import tpu as pltpu

EPS = 1e-6
LRU_C = 8.0
N_HEADS = 8
TOPK = 16
N_KEYS = 128
SHORT_K = 3
LRU_K = 4
HALO = 8

SUBLANES = 8
LANES = 128
VMEM_LIMIT_BYTES = 56 * 1024 * 1024

W_PITCH = N_KEYS // 2 + SUBLANES
BUILD_UNROLL = 64
MIXER_ROWS = 512
EXPERT_TOKENS = 512
EXPERT_CHUNK = 1024
ROUTE_LANES = 256

F32 = jnp.float32
BF16 = jnp.bfloat16
NT_DIMS = (((1,), (1,)), ((), ()))


def _dot(a, b):
    return jnp.dot(a, b, preferred_element_type=F32)


def _dot_nt(a, b, precision=None):
    return lax.dot_general(a, b, NT_DIMS, preferred_element_type=F32, precision=precision)


def _rms(x, g):
    return x * lax.rsqrt(jnp.mean(x * x, axis=-1, keepdims=True) + EPS) * g


def _adaln_kernel(c_ref, w_ref, b_ref, o_ref):
    c = c_ref[...]
    c_act = c * jax.nn.sigmoid(c)
    o_ref[...] = jnp.dot(c_act, w_ref[...], preferred_element_type=F32,
                         precision=lax.Precision.HIGHEST) + b_ref[...]


def _adaln(c, w, b):
    bsz, d = c.shape
    n = w.shape[1]
    blk = d
    return pl.pallas_call(
        _adaln_kernel,
        grid=(n // blk,),
        in_specs=[pl.BlockSpec((bsz, d), lambda j: (0, 0)),
                  pl.BlockSpec((d, blk), lambda j: (0, j)),
                  pl.BlockSpec((1, blk), lambda j: (0, j))],
        out_specs=pl.BlockSpec((bsz, blk), lambda j: (0, j)),
        out_shape=jax.ShapeDtypeStruct((bsz, n), F32),
        compiler_params=pltpu.CompilerParams(vmem_limit_bytes=VMEM_LIMIT_BYTES),
        name="adaln",
    )(c, w, b.reshape(1, n))


def _causal_conv(buf_ref, v, w_ref, ts, k_w):
    buf_ref[HALO:HALO + ts, :] = v
    out = v * w_ref[k_w - 1:k_w, :]
    for k in range(k_w - 1):
        shift = k_w - 1 - k
        out = out + buf_ref[HALO - shift:HALO - shift + ts, :] * w_ref[k:k + 1, :]
    buf_ref[0:HALO, :] = v[ts - HALO:ts, :]
    return out


def _head_ms(y, m_ref):
    sq = y * y
    hi = sq.astype(BF16)
    lo = (sq - hi.astype(F32)).astype(BF16)
    return _dot(hi, m_ref[...]) + _dot(lo, m_ref[...])


def _mixer_kernel(x_ref, ada_ref, n1g_ref, win_ref, caw_ref, cbw_ref, cbb_ref, wg_ref, bg_ref,
                  lam_ref, gna_ref, gnb_ref, hm_ref, wout_ref, n2g_ref,
                  x1_ref, h2_ref, bufa_ref, bufb_ref, carry_ref, *, ts, cw, rows):
    @pl.when(pl.program_id(1) == 0)
    def _():
        bufa_ref[:, 0:HALO, :] = jnp.zeros((rows, HALO, cw), F32)
        bufb_ref[:, 0:HALO, :] = jnp.zeros((rows, HALO, cw), F32)
        carry_ref[...] = jnp.zeros_like(carry_ref)

    for c in range(rows):
        _mixer_chain(x_ref.at[c], ada_ref.at[c], n1g_ref, win_ref, caw_ref, cbw_ref, cbb_ref,
                     wg_ref, bg_ref, lam_ref, gna_ref, gnb_ref, hm_ref, wout_ref, n2g_ref,
                     x1_ref.at[c], h2_ref.at[c], bufa_ref.at[c], bufb_ref.at[c], carry_ref.at[c],
                     ts=ts, cw=cw)


def _mixer_chain(x_ref, ada_ref, n1g_ref, win_ref, caw_ref, cbw_ref, cbb_ref, wg_ref, bg_ref,
                 lam_ref, gna_ref, gnb_ref, hm_ref, wout_ref, n2g_ref,
                 x1_ref, h2_ref, bufa_ref, bufb_ref, carry_ref, *, ts, cw):
    x = x_ref[...]
    ada = ada_ref[...]
    sh1, sc1, g1 = ada[0:1], ada[1:2], ada[2:3]
    sh2, sc2 = ada[3:4], ada[4:5]

    h = _rms(x, n1g_ref[...]) * (1.0 + sc1) + sh1
    z = _dot(h.astype(BF16), win_ref[...])
    gate_b = z[:, 0:cw]
    gate_c = z[:, cw:2 * cw]
    xa = z[:, 2 * cw:3 * cw]
    xr = z[:, 3 * cw:4 * cw]
    gr = z[:, 4 * cw:5 * cw]

    y_a = gate_b * _causal_conv(bufa_ref, gate_c * xa, caw_ref, ts, SHORT_K)

    xc = _causal_conv(bufb_ref, xr, cbw_ref, ts, LRU_K) + cbb_ref[...]
    pre = _dot(xc.astype(BF16), wg_ref[...]) + bg_ref[...]
    r = jax.nn.sigmoid(pre[:, 0:cw])
    i = jax.nn.sigmoid(pre[:, cw:2 * cw])
    nl = -lam_ref[...]
    softplus = jnp.maximum(nl, 0.0) + jnp.log1p(jnp.exp(-jnp.abs(nl)))
    log_a = (-LRU_C) * r * softplus
    a = jnp.exp(log_a)
    u = jnp.sqrt(-jnp.tanh(log_a) * (a * a + 1.0)) * (i * xc)

    ng = ts // SUBLANES
    acc_a = a.reshape(ng, SUBLANES, cw)
    acc_b = u.reshape(ng, SUBLANES, cw)
    sub = lax.broadcasted_iota(jnp.int32, (ng, SUBLANES, cw), 1)
    d = 1
    while d < SUBLANES:
        keep = sub >= d
        a_sh = jnp.where(keep, pltpu.roll(acc_a, d, 1), 1.0)
        b_sh = jnp.where(keep, pltpu.roll(acc_b, d, 1), 0.0)
        acc_b = acc_a * b_sh + acc_b
        acc_a = acc_a * a_sh
        d *= 2
    state = carry_ref[0:1, :]
    groups = []
    for g in range(ng):
        hg = acc_a[g] * state + acc_b[g]
        groups.append(hg)
        state = hg[SUBLANES - 1:SUBLANES, :]
    hseq = jnp.concatenate(groups, axis=0)
    carry_ref[0:1, :] = state

    y_b = hseq * jax.nn.gelu(gr, approximate=True)

    na = y_a * lax.rsqrt(_head_ms(y_a, hm_ref) + EPS) * gna_ref[...]
    nb = y_b * lax.rsqrt(_head_ms(y_b, hm_ref) + EPS) * gnb_ref[...]
    y = _dot(na.astype(BF16), wout_ref[0:cw, :]) + _dot(nb.astype(BF16), wout_ref[cw:2 * cw, :])

    x1 = x + g1 * y
    x1_ref[...] = x1
    h2_ref[...] = (_rms(x1, n2g_ref[...]) * (1.0 + sc2) + sh2).astype(BF16)


def _mixer(x, ada, n1g, w_in, caw, cbw, cbb, w_gate, b_gate, lam, gna, gnb, head_mean, w_out, n2g):
    bsz, seq, d = x.shape
    cw = caw.shape[1]
    ts = min(MIXER_ROWS, seq)
    rows = 2 if bsz % 2 == 0 else 1
    assert seq % ts == 0 and ts % SUBLANES == 0 and ts >= HALO
    full = lambda a: pl.BlockSpec(a.shape, lambda b, s: (0,) * a.ndim)
    kern = functools.partial(_mixer_kernel, ts=ts, cw=cw, rows=rows)
    return pl.pallas_call(
        kern,
        grid=(bsz // rows, seq // ts),
        in_specs=[pl.BlockSpec((rows, ts, d), lambda b, s: (b, s, 0)),
                  pl.BlockSpec((rows,) + ada.shape[1:], lambda b, s: (b, 0, 0)),
                  full(n1g), full(w_in), full(caw), full(cbw), full(cbb), full(w_gate),
                  full(b_gate), full(lam), full(gna), full(gnb), full(head_mean), full(w_out),
                  full(n2g)],
        out_specs=[pl.BlockSpec((rows, ts, d), lambda b, s: (b, s, 0)),
                   pl.BlockSpec((rows, ts, d), lambda b, s: (b, s, 0))],
        out_shape=[jax.ShapeDtypeStruct((bsz, seq, d), F32),
                   jax.ShapeDtypeStruct((bsz, seq, d), BF16)],
        scratch_shapes=[pltpu.VMEM((rows, ts + HALO, cw), F32),
                        pltpu.VMEM((rows, ts + HALO, cw), F32),
                        pltpu.VMEM((rows, SUBLANES, cw), F32)],
        compiler_params=pltpu.CompilerParams(
            dimension_semantics=("arbitrary", "arbitrary"),
            vmem_limit_bytes=VMEM_LIMIT_BYTES),
        name="mixer",
    )(x, ada, n1g, w_in, caw, cbw, cbb, w_gate, b_gate, lam, gna, gnb, head_mean, w_out, n2g)


def _keyfold_kernel(k_ref, wqt_ref, o_ref):
    o_ref[0] = jnp.dot(k_ref[0], wqt_ref[0], preferred_element_type=F32,
                       precision=lax.Precision.HIGHEST).astype(BF16)


def _keyfold(keys, wq_t):
    g, n, dk = keys.shape
    d = wq_t.shape[2]
    return pl.pallas_call(
        _keyfold_kernel,
        grid=(g,),
        in_specs=[pl.BlockSpec((1, n, dk), lambda i: (i, 0, 0)),
                  pl.BlockSpec((1, dk, d), lambda i: (i, 0, 0))],
        out_specs=pl.BlockSpec((1, n, d), lambda i: (i, 0, 0)),
        out_shape=jax.ShapeDtypeStruct((g, n, d), BF16),
        compiler_params=pltpu.CompilerParams(vmem_limit_bytes=VMEM_LIMIT_BYTES),
        name="keyfold",
    )(keys, wq_t)


def _candidate_layout():
    groups = []
    ra = row0 = 0
    while ra < TOPK:
        nb = TOPK // (ra + 1)
        if nb > 1:
            kind, valid, step = "row", nb, 1
        else:
            kind, valid, step = "col", TOPK - ra, TOPK - ra
        rows = -(-valid // SUBLANES) * SUBLANES
        groups.append((kind, ra, valid, rows, row0))
        ra += step
        row0 += rows
    return groups


_CAND_GROUPS = _candidate_layout()
_BIG = float(1 << 20)


def _top16(s):
    nb, n = s.shape[0] // SUBLANES, s.shape[1]
    blocks = [s[SUBLANES * v:SUBLANES * (v + 1), :] for v in range(nb)]
    sub = lax.broadcasted_iota(jnp.int32, (SUBLANES, n), 0).astype(F32)
    vals, pos = [], []
    for _ in range(TOPK):
        best = blocks[0]
        first = jnp.zeros((SUBLANES, n), F32)
        for v in range(1, nb):
            first = jnp.where(blocks[v] > best, float(v), first)
            best = jnp.maximum(best, blocks[v])
        m = jnp.max(best, axis=0, keepdims=True)
        p = jnp.min(jnp.where(best == m, first * float(SUBLANES) + sub, _BIG), axis=0, keepdims=True)
        off = p - sub
        blocks = [jnp.where(off == float(SUBLANES * v), -jnp.inf, blocks[v]) for v in range(nb)]
        vals.append(m)
        pos.append(p)
    return jnp.concatenate(vals, axis=0), jnp.concatenate(pos, axis=0)


def _pick_rank(table, rk):
    out = jnp.zeros_like(table)
    for r in range(TOPK):
        out = jnp.where(rk == r, table[r:r + 1, :], out)
    return out


def _batcher_pairs(n):
    pairs = []
    p = 1
    while p < n:
        k = p
        while k >= 1:
            for j in range(k % p, n - k, 2 * k):
                for i in range(min(k, n - j - k)):
                    if (i + j) // (2 * p) == (i + j + k) // (2 * p):
                        pairs.append((i + j, i + j + k))
            k //= 2
        p *= 2
    return pairs


_SORT_PAIRS = _batcher_pairs(N_KEYS // SUBLANES)


def _merge_top(groups, n_out):
    out_v, out_i = [], []
    for r in range(n_out):
        heads = [(g[0][0], g[1][0]) for g in groups]
        top = heads[0][0]
        for hv, _ in heads[1:]:
            top = jnp.maximum(top, hv)
        m = jnp.max(top, axis=0, keepdims=True)
        cand = None
        for hv, hi in heads:
            c = jnp.where(hv == m, hi, _BIG)
            cand = c if cand is None else jnp.minimum(cand, c)
        p = jnp.min(cand, axis=0, keepdims=True)
        out_v.append(m)
        out_i.append(p)
        left = n_out - 1 - r
        if left == 0:
            break
        for vals, ids in groups:
            hit = ids[0] == p
            depth = len(vals)
            for d in range(min(depth - 1, left)):
                vals[d] = jnp.where(hit, vals[d + 1], vals[d])
                ids[d] = jnp.where(hit, ids[d + 1], ids[d])
            if depth - 1 < left:
                vals[depth - 1] = jnp.where(hit, -jnp.inf, vals[depth - 1])
    return jnp.concatenate(out_v, axis=0), jnp.concatenate(out_i, axis=0), out_v


def _top16_sorted(s):
    nb, n = s.shape[0] // SUBLANES, s.shape[1]
    assert nb == N_KEYS // SUBLANES
    sub = lax.broadcasted_iota(jnp.int32, (SUBLANES, n), 0).astype(F32)
    vals = [s[SUBLANES * v:SUBLANES * (v + 1), :] for v in range(nb)]
    ids = [sub + float(SUBLANES * v) for v in range(nb)]
    for i, j in _SORT_PAIRS:
        swap = vals[j] > vals[i]
        vals[i], vals[j] = jnp.maximum(vals[i], vals[j]), jnp.minimum(vals[i], vals[j])
        ids[i], ids[j] = jnp.where(swap, ids[j], ids[i]), jnp.where(swap, ids[i], ids[j])
    top_v, top_i, rows = _merge_top([[vals, ids]], TOPK + 1)
    ok = top_v[0:TOPK, :] > jnp.concatenate(rows[1:], axis=0)
    return top_v[0:TOPK, :], top_i[0:TOPK, :], ok


def _route_head_fast(st):
    n = st.shape[1]
    s1, i1, ok1 = _top16_sorted(st[0:N_KEYS, :])
    s2, i2, ok2 = _top16_sorted(st[N_KEYS:2 * N_KEYS, :])
    sub = lax.broadcasted_iota(jnp.int32, (SUBLANES, n), 0)
    sub_f = sub.astype(F32)
    head = s1[0:1, :] + s2
    lo_v, lo_i = [head[0:SUBLANES, :]], [sub_f]
    for ra in range(1, TOPK):
        nb = TOPK // (ra + 1)
        v = s1[ra:ra + 1, :] + s2[0:SUBLANES, :]
        c = sub_f + float(ra * TOPK)
        if nb < SUBLANES:
            v = jnp.where(sub < nb, v, -jnp.inf)
            c = jnp.where(sub < nb, c, _BIG)
        lo_v.append(v)
        lo_i.append(c)
    hi = [[head[SUBLANES:TOPK, :]], [sub_f + float(SUBLANES)]]
    best_s, best_c, _ = _merge_top([[lo_v, lo_i], hi], TOPK)
    best_c = best_c.astype(jnp.int32)
    ra_sel = lax.shift_right_logical(best_c, 4).astype(F32)
    rb_sel = jnp.bitwise_and(best_c, TOPK - 1).astype(F32)
    e = jnp.exp(best_s - best_s[0:1, :])
    g = e / jnp.sum(e, axis=0, keepdims=True)
    ok = jnp.where(ok1, 0.0, 1.0) + jnp.where(ok2, 0.0, 1.0)
    return _pick_rank(i1, ra_sel), _pick_rank(i2, rb_sel), g, jnp.max(ok) == 0.0


def _route_head(st):
    n = st.shape[1]
    s1, i1 = _top16(st[0:N_KEYS, :])
    s2, i2 = _top16(st[N_KEYS:2 * N_KEYS, :])
    sub = lax.broadcasted_iota(jnp.int32, (SUBLANES, n), 0)
    parts = []
    for kind, ra, valid, rows, _ in _CAND_GROUPS:
        if kind == "row":
            part = s1[ra:ra + 1, :] + s2[0:rows, :]
        else:
            part = s1[ra:ra + rows, :] + s2[0:1, :]
        if valid < rows:
            assert rows == SUBLANES
            part = jnp.where(sub < valid, part, -jnp.inf)
        parts.append(part)
    best_s, best_p = _top16(jnp.concatenate(parts, axis=0))
    ra_sel = jnp.zeros_like(best_p)
    rb_sel = jnp.zeros_like(best_p)
    for kind, ra, valid, rows, row0 in _CAND_GROUPS:
        in_group = best_p >= float(row0)
        local = best_p - float(row0)
        if kind == "row":
            ra_sel = jnp.where(in_group, float(ra), ra_sel)
            rb_sel = jnp.where(in_group, local, rb_sel)
        else:
            ra_sel = jnp.where(in_group, local + float(ra), ra_sel)
            rb_sel = jnp.where(in_group, 0.0, rb_sel)
    e = jnp.exp(best_s - best_s[0:1, :])
    g = e / jnp.sum(e, axis=0, keepdims=True)
    return _pick_rank(i1, ra_sel), _pick_rank(i2, rb_sel), g


def _route_kernel(h_ref, ws_ref, a_ref, b_ref, g_ref):
    st = _dot_nt(ws_ref[...], h_ref[...])
    a_all, b_all, g_all = [], [], []
    for hd in range(N_HEADS):
        a, b, g = _route_head(st[hd * 2 * N_KEYS:(hd + 1) * 2 * N_KEYS, :])
        a_all.append(a)
        b_all.append(b)
        g_all.append(g)
    a_ref[...] = jnp.concatenate(a_all, axis=0).T.astype(jnp.int32)
    b_ref[...] = jnp.concatenate(b_all, axis=0).T.astype(jnp.int32)
    g_ref[...] = jnp.concatenate(g_all, axis=0).T


def _route(h2, ws):
    t, d = h2.shape
    tt = min(ROUTE_LANES, t)
    assert t % tt == 0
    hk = N_HEADS * TOPK
    out = jax.ShapeDtypeStruct((t, hk), jnp.int32)
    return pl.pallas_call(
        _route_kernel,
        grid=(t // tt,),
        in_specs=[pl.BlockSpec((tt, d), lambda i: (i, 0)),
                  pl.BlockSpec(ws.shape, lambda i: (0, 0))],
        out_specs=[pl.BlockSpec((tt, hk), lambda i: (i, 0))] * 3,
        out_shape=[out, out, jax.ShapeDtypeStruct((t, hk), F32)],
        compiler_params=pltpu.CompilerParams(
            dimension_semantics=("arbitrary",), vmem_limit_bytes=VMEM_LIMIT_BYTES),
        name="route",
    )(h2, ws)


def _experts_kernel(h_ref, hn_ref, ws_ref, a0_ref, b0_ref, g0_ref, u_ref, v_ref,
                    x1_ref, ada_ref, fg_ref, o_ref,
                    w3_ref, acc_ref, coef_ref, a_ref, b_ref, g_ref, at_ref, bt_ref, gt_ref,
                    *, tt, ec, n_chunks, n_tiles, final_norm):
    s = pl.program_id(0)
    i = s // n_chunks
    j = s % n_chunks
    rows_per_step = ec // N_KEYS
    half = tt // 2

    @pl.when(s == 0)
    def _():
        a_ref[...] = a0_ref[...]
        b_ref[...] = b0_ref[...]
        g_ref[...] = g0_ref[...]
        acc_ref[...] = jnp.zeros_like(acc_ref)
        coef_ref[...] = jnp.zeros_like(coef_ref)

    @pl.when((j == 0) & (i < n_tiles))
    def _():
        sub = lax.broadcasted_iota(jnp.int32, (N_KEYS, a_ref.shape[1]), 0)

        def build(t, carry):
            arow = a_ref[pl.ds(t, 1), :]
            brow = b_ref[pl.ds(t, 1), :]
            grow = 0.5 * g_ref[pl.ds(t, 1), :]
            pt = jnp.where(sub == arow, grow, 0.0).astype(BF16)
            qt = jnp.where(sub == brow, 1.0, 0.0).astype(BF16)
            wt = _dot_nt(pt, qt).astype(BF16)
            off = pl.multiple_of(t * W_PITCH, SUBLANES)
            w3_ref[pl.ds(off, N_KEYS // 2), :] = pltpu.bitcast(wt, jnp.uint32)
            return carry

        lax.fori_loop(0, tt, build, 0, unroll=BUILD_UNROLL)

    upc = 2 * N_HEADS // n_chunks
    tok_half = (j * upc) // N_HEADS
    hd0 = (j * upc) % N_HEADS
    hn_rows = pl.ds(pl.multiple_of(tok_half * half, half), half)
    scores = _dot_nt(ws_ref[...], hn_ref[hn_rows, :])

    acc_ref[...] += _dot(coef_ref[(s + 1) % 2], v_ref[...])

    act = _dot_nt(h_ref[...], u_ref[...])
    gel = act * (1.0 + lax.erf(act * np.float32(1.0 / np.sqrt(2.0))))
    parts = []
    for r in range(rows_per_step // 2):
        word = w3_ref[pl.ds(j * (rows_per_step // 2) + r, tt, stride=W_PITCH), :]
        w_even = lax.bitcast_convert_type(lax.shift_left(word, jnp.uint32(16)), F32)
        w_odd = lax.bitcast_convert_type(jnp.bitwise_and(word, jnp.uint32(0xFFFF0000)), F32)
        k = 2 * r
        parts.append((w_even * gel[:, k * N_KEYS:(k + 1) * N_KEYS]).astype(BF16))
        parts.append((w_odd * gel[:, (k + 1) * N_KEYS:(k + 2) * N_KEYS]).astype(BF16))
    coef_ref[s % 2] = jnp.concatenate(parts, axis=1)

    all_ok = None
    for q in range(upc):
        ra, rb, rg, fast_ok = _route_head_fast(scores[q * 2 * N_KEYS:(q + 1) * 2 * N_KEYS, :])
        at_ref[tok_half, hd0 + q] = ra
        bt_ref[tok_half, hd0 + q] = rb
        gt_ref[tok_half, hd0 + q] = rg
        all_ok = fast_ok if all_ok is None else jnp.logical_and(all_ok, fast_ok)

    @pl.when(jnp.logical_not(all_ok))
    def _():
        exact_scores = _dot_nt(ws_ref[...], hn_ref[hn_rows, :])
        for q in range(upc):
            ra, rb, rg = _route_head(exact_scores[q * 2 * N_KEYS:(q + 1) * 2 * N_KEYS, :])
            at_ref[tok_half, hd0 + q] = ra
            bt_ref[tok_half, hd0 + q] = rb
            gt_ref[tok_half, hd0 + q] = rg

    @pl.when((j == 0) & (s > 0))
    def _():
        g2 = ada_ref[0][5:6]
        x2 = x1_ref[...] + g2 * acc_ref[...]
        if final_norm:
            x2 = _rms(x2, fg_ref[...])
        o_ref[...] = x2
        acc_ref[...] = jnp.zeros_like(acc_ref)

    @pl.when(j == n_chunks - 1)
    def _():
        hk = N_HEADS * TOPK
        for hf in range(2):
            rows = slice(hf * half, (hf + 1) * half)
            a_ref[rows, :] = at_ref[hf].reshape(hk, half).T.astype(jnp.int32)
            b_ref[rows, :] = bt_ref[hf].reshape(hk, half).T.astype(jnp.int32)
            g_ref[rows, :] = gt_ref[hf].reshape(hk, half).T


def _experts(h2, ws, abg0, u, v, x1, ada, final_g, seq, final_norm):
    t, d = h2.shape
    ne = u.shape[0]
    hk = N_HEADS * TOPK
    tt = min(EXPERT_TOKENS, seq)
    ec = EXPERT_CHUNK
    half = tt // 2
    n_tiles = t // tt
    nc = ne // ec
    upc = 2 * N_HEADS // nc
    assert t % tt == 0 and seq % tt == 0 and ne == N_KEYS * N_KEYS and ne % ec == 0
    assert ec % (2 * N_KEYS) == 0 and upc * nc == 2 * N_HEADS and N_HEADS % upc == 0
    assert tt % BUILD_UNROLL == 0
    kern = functools.partial(_experts_kernel, tt=tt, ec=ec, n_chunks=nc, n_tiles=n_tiles,
                             final_norm=final_norm)
    tile = lambda s: jnp.minimum(s // nc, n_tiles - 1)
    done = lambda s: jnp.maximum(s - 1, 0) // nc
    first = lambda w: pl.BlockSpec((tt, w), lambda s: (0, 0), pipeline_mode=pl.Buffered(1))
    scr_t = pltpu.VMEM((2, N_HEADS, TOPK, half), F32)
    return pl.pallas_call(
        kern,
        grid=(n_tiles * nc + 1,),
        in_specs=[pl.BlockSpec((tt, d), lambda s: (tile(s), 0)),
                  pl.BlockSpec((tt, d), lambda s: (jnp.minimum(s // nc + 1, n_tiles - 1), 0)),
                  pl.BlockSpec((upc * 2 * N_KEYS, d), lambda s: (s % (N_HEADS // upc), 0)),
                  first(hk), first(hk), first(hk),
                  pl.BlockSpec((ec, d), lambda s: (s % nc, 0)),
                  pl.BlockSpec((ec, d), lambda s: ((s + nc - 1) % nc, 0)),
                  pl.BlockSpec((tt, d), lambda s: (done(s), 0)),
                  pl.BlockSpec((1,) + ada.shape[1:], lambda s: ((done(s) * tt) // seq, 0, 0)),
                  pl.BlockSpec(final_g.shape, lambda s: (0, 0))],
        out_specs=pl.BlockSpec((tt, d), lambda s: (done(s), 0)),
        out_shape=jax.ShapeDtypeStruct((t, d), F32),
        scratch_shapes=[pltpu.VMEM((tt * W_PITCH, N_KEYS), jnp.uint32),
                        pltpu.VMEM((tt, d), F32),
                        pltpu.VMEM((2, tt, ec), BF16),
                        pltpu.VMEM((tt, hk), jnp.int32),
                        pltpu.VMEM((tt, hk), jnp.int32),
                        pltpu.VMEM((tt, hk), F32),
                        scr_t, scr_t, scr_t],
        compiler_params=pltpu.CompilerParams(
            dimension_semantics=("arbitrary",),
            vmem_limit_bytes=VMEM_LIMIT_BYTES),
        name="experts",
    )(h2, h2, ws, *abg0, u, v, x1, ada, final_g)


def _block_diag(w):
    nh, hd, _ = w.shape
    eye = jnp.eye(nh, dtype=w.dtype)
    return (eye[:, None, :, None] * w[:, :, None, :]).reshape(nh * hd, nh * hd)


def kernel(x, c, w_ada, b_ada, norm1_g, w_in, conv_a_w, conv_b_w, conv_b_b, w_r, b_r, w_i, b_i,
           lru_lambda, gn_a, gn_b, w_out, norm2_g, w_q, sub_keys, expert_u, expert_v, final_g):
    bsz, seq, d = x.shape
    depth = w_ada.shape[0]
    cw = conv_a_w.shape[2]
    hd = cw // N_HEADS
    head_mean = _block_diag(jnp.full((N_HEADS, hd, hd), 1.0 / hd, F32)).astype(BF16)
    row = lambda p: p.reshape(1, -1)
    for l in range(depth):
        ada = _adaln(c, w_ada[l], b_ada[l]).reshape(bsz, 6, d)
        w_gate = jnp.concatenate([_block_diag(w_r[l]), _block_diag(w_i[l])], axis=1).astype(BF16)
        b_gate = jnp.concatenate([row(b_r[l]), row(b_i[l])], axis=1)
        x1, h2 = _mixer(x, ada, row(norm1_g[l]), w_in[l].astype(BF16), conv_a_w[l], conv_b_w[l],
                        row(conv_b_b[l]), w_gate, b_gate, row(lru_lambda[l]), row(gn_a[l]),
                        row(gn_b[l]), head_mean, w_out[l].astype(BF16), row(norm2_g[l]))
        dk = sub_keys.shape[-1]
        keys = sub_keys[l].reshape(N_HEADS * 2, N_KEYS, dk)
        wq_t = w_q[l].T.reshape(N_HEADS * 2, dk, d)
        ws = _keyfold(keys, wq_t).reshape(N_HEADS * 2 * N_KEYS, d)
        h2f = h2.reshape(bsz * seq, d)
        abg0 = _route(h2f[:min(EXPERT_TOKENS, seq)], ws)
        x = _experts(h2f, ws, abg0, expert_u[l].astype(BF16), expert_v[l].astype(BF16),
                     x1.reshape(bsz * seq, d), ada, row(final_g), seq,
                     final_norm=(l == depth - 1)).reshape(bsz, seq, d)
    return x
```

```python
import functools

import numpy as np
import jax
import jax.numpy as jnp
from jax import lax
from jax.experimental import pallas as pl
from jax.experimental.pallas import tpu as pltpu

EPS = 1e-6
LRU_C = 8.0
N_HEADS = 8
TOPK = 16
N_KEYS = 128
SHORT_K = 3
LRU_K = 4
HALO = 8

SUBLANES = 8
LANES = 128
VMEM_LIMIT_BYTES = 56 * 1024 * 1024

W_PITCH = N_KEYS // 2 + SUBLANES
BUILD_UNROLL = 128
MIXER_ROWS = 512
EXPERT_TOKENS = 512
EXPERT_CHUNK = 1024
ROUTE_LANES = 256

F32 = jnp.float32
BF16 = jnp.bfloat16
NT_DIMS = (((1,), (1,)), ((), ()))


def _dot(a, b):
    return jnp.dot(a, b, preferred_element_type=F32)


def _dot_nt(a, b, precision=None):
    return lax.dot_general(a, b, NT_DIMS, preferred_element_type=F32, precision=precision)


def _rms(x, g):
    return x * lax.rsqrt(jnp.mean(x * x, axis=-1, keepdims=True) + EPS) * g


def _adaln_kernel(c_ref, w_ref, b_ref, o_ref):
    c = c_ref[...]
    c_act = c * jax.nn.sigmoid(c)
    o_ref[...] = jnp.dot(c_act, w_ref[...], preferred_element_type=F32,
                         precision=lax.Precision.HIGHEST) + b_ref[...]


def _adaln(c, w, b):
    bsz, d = c.shape
    n = w.shape[1]
    blk = d
    return pl.pallas_call(
        _adaln_kernel,
        grid=(n // blk,),
        in_specs=[pl.BlockSpec((bsz, d), lambda j: (0, 0)),
                  pl.BlockSpec((d, blk), lambda j: (0, j)),
                  pl.BlockSpec((1, blk), lambda j: (0, j))],
        out_specs=pl.BlockSpec((bsz, blk), lambda j: (0, j)),
        out_shape=jax.ShapeDtypeStruct((bsz, n), F32),
        compiler_params=pltpu.CompilerParams(vmem_limit_bytes=VMEM_LIMIT_BYTES),
        name="adaln",
    )(c, w, b.reshape(1, n))


def _causal_conv(buf_ref, v, w_ref, ts, k_w):
    buf_ref[HALO:HALO + ts, :] = v
    out = v * w_ref[k_w - 1:k_w, :]
    for k in range(k_w - 1):
        shift = k_w - 1 - k
        out = out + buf_ref[HALO - shift:HALO - shift + ts, :] * w_ref[k:k + 1, :]
    buf_ref[0:HALO, :] = v[ts - HALO:ts, :]
    return out


def _head_ms(y, m_ref):
    sq = y * y
    hi = sq.astype(BF16)
    lo = (sq - hi.astype(F32)).astype(BF16)
    return _dot(hi, m_ref[...]) + _dot(lo, m_ref[...])


def _mixer_kernel(x_ref, ada_ref, n1g_ref, win_ref, caw_ref, cbw_ref, cbb_ref, wg_ref, bg_ref,
                  lam_ref, gna_ref, gnb_ref, hm_ref, wout_ref, n2g_ref,
                  x1_ref, h2_ref, bufa_ref, bufb_ref, carry_ref, *, ts, cw, rows):
    @pl.when(pl.program_id(1) == 0)
    def _():
        bufa_ref[:, 0:HALO, :] = jnp.zeros((rows, HALO, cw), F32)
        bufb_ref[:, 0:HALO, :] = jnp.zeros((rows, HALO, cw), F32)
        carry_ref[...] = jnp.zeros_like(carry_ref)

    for c in range(rows):
        _mixer_chain(x_ref.at[c], ada_ref.at[c], n1g_ref, win_ref, caw_ref, cbw_ref, cbb_ref,
                     wg_ref, bg_ref, lam_ref, gna_ref, gnb_ref, hm_ref, wout_ref, n2g_ref,
                     x1_ref.at[c], h2_ref.at[c], bufa_ref.at[c], bufb_ref.at[c], carry_ref.at[c],
                     ts=ts, cw=cw)


def _mixer_chain(x_ref, ada_ref, n1g_ref, win_ref, caw_ref, cbw_ref, cbb_ref, wg_ref, bg_ref,
                 lam_ref, gna_ref, gnb_ref, hm_ref, wout_ref, n2g_ref,
                 x1_ref, h2_ref, bufa_ref, bufb_ref, carry_ref, *, ts, cw):
    x = x_ref[...]
    ada = ada_ref[...]
    sh1, sc1, g1 = ada[0:1], ada[1:2], ada[2:3]
    sh2, sc2 = ada[3:4], ada[4:5]

    h = _rms(x, n1g_ref[...]) * (1.0 + sc1) + sh1
    z = _dot(h.astype(BF16), win_ref[...])
    gate_b = z[:, 0:cw]
    gate_c = z[:, cw:2 * cw]
    xa = z[:, 2 * cw:3 * cw]
    xr = z[:, 3 * cw:4 * cw]
    gr = z[:, 4 * cw:5 * cw]

    y_a = gate_b * _causal_conv(bufa_ref, gate_c * xa, caw_ref, ts, SHORT_K)

    xc = _causal_conv(bufb_ref, xr, cbw_ref, ts, LRU_K) + cbb_ref[...]
    pre = _dot(xc.astype(BF16), wg_ref[...]) + bg_ref[...]
    r = jax.nn.sigmoid(pre[:, 0:cw])
    i = jax.nn.sigmoid(pre[:, cw:2 * cw])
    nl = -lam_ref[...]
    softplus = jnp.maximum(nl, 0.0) + jnp.log1p(jnp.exp(-jnp.abs(nl)))
    log_a = (-LRU_C) * r * softplus
    a = jnp.exp(log_a)
    u = jnp.sqrt(-jnp.tanh(log_a) * (a * a + 1.0)) * (i * xc)

    ng = ts // SUBLANES
    acc_a = a.reshape(ng, SUBLANES, cw)
    acc_b = u.reshape(ng, SUBLANES, cw)
    sub = lax.broadcasted_iota(jnp.int32, (ng, SUBLANES, cw), 1)
    d = 1
    while d < SUBLANES:
        keep = sub >= d
        a_sh = jnp.where(keep, pltpu.roll(acc_a, d, 1), 1.0)
        b_sh = jnp.where(keep, pltpu.roll(acc_b, d, 1), 0.0)
        acc_b = acc_a * b_sh + acc_b
        acc_a = acc_a * a_sh
        d *= 2
    state = carry_ref[0:1, :]
    groups = []
    for g in range(ng):
        hg = acc_a[g] * state + acc_b[g]
        groups.append(hg)
        state = hg[SUBLANES - 1:SUBLANES, :]
    hseq = jnp.concatenate(groups, axis=0)
    carry_ref[0:1, :] = state

    y_b = hseq * jax.nn.gelu(gr, approximate=True)

    na = y_a * lax.rsqrt(_head_ms(y_a, hm_ref) + EPS) * gna_ref[...]
    nb = y_b * lax.rsqrt(_head_ms(y_b, hm_ref) + EPS) * gnb_ref[...]
    y = _dot(na.astype(BF16), wout_ref[0:cw, :]) + _dot(nb.astype(BF16), wout_ref[cw:2 * cw, :])

    x1 = x + g1 * y
    x1_ref[...] = x1
    h2_ref[...] = (_rms(x1, n2g_ref[...]) * (1.0 + sc2) + sh2).astype(BF16)


def _mixer(x, ada, n1g, w_in, caw, cbw, cbb, w_gate, b_gate, lam, gna, gnb, head_mean, w_out, n2g):
    bsz, seq, d = x.shape
    cw = caw.shape[1]
    ts = min(MIXER_ROWS, seq)
    rows = 2 if bsz % 2 == 0 else 1
    assert seq % ts == 0 and ts % SUBLANES == 0 and ts >= HALO
    full = lambda a: pl.BlockSpec(a.shape, lambda b, s: (0,) * a.ndim)
    kern = functools.partial(_mixer_kernel, ts=ts, cw=cw, rows=rows)
    return pl.pallas_call(
        kern,
        grid=(bsz // rows, seq // ts),
        in_specs=[pl.BlockSpec((rows, ts, d), lambda b, s: (b, s, 0)),
                  pl.BlockSpec((rows,) + ada.shape[1:], lambda b, s: (b, 0, 0)),
                  full(n1g), full(w_in), full(caw), full(cbw), full(cbb), full(w_gate),
                  full(b_gate), full(lam), full(gna), full(gnb), full(head_mean), full(w_out),
                  full(n2g)],
        out_specs=[pl.BlockSpec((rows, ts, d), lambda b, s: (b, s, 0)),
                   pl.BlockSpec((rows, ts, d), lambda b, s: (b, s, 0))],
        out_shape=[jax.ShapeDtypeStruct((bsz, seq, d), F32),
                   jax.ShapeDtypeStruct((bsz, seq, d), BF16)],
        scratch_shapes=[pltpu.VMEM((rows, ts + HALO, cw), F32),
                        pltpu.VMEM((rows, ts + HALO, cw), F32),
                        pltpu.VMEM((rows, SUBLANES, cw), F32)],
        compiler_params=pltpu.CompilerParams(
            dimension_semantics=("arbitrary", "arbitrary"),
            vmem_limit_bytes=VMEM_LIMIT_BYTES),
        name="mixer",
    )(x, ada, n1g, w_in, caw, cbw, cbb, w_gate, b_gate, lam, gna, gnb, head_mean, w_out, n2g)


def _keyfold_kernel(k_ref, wqt_ref, o_ref):
    o_ref[0] = jnp.dot(k_ref[0], wqt_ref[0], preferred_element_type=F32,
                       precision=lax.Precision.HIGHEST).astype(BF16)


def _keyfold(keys, wq_t):
    g, n, dk = keys.shape
    d = wq_t.shape[2]
    return pl.pallas_call(
        _keyfold_kernel,
        grid=(g,),
        in_specs=[pl.BlockSpec((1, n, dk), lambda i: (i, 0, 0)),
                  pl.BlockSpec((1, dk, d), lambda i: (i, 0, 0))],
        out_specs=pl.BlockSpec((1, n, d), lambda i: (i, 0, 0)),
        out_shape=jax.ShapeDtypeStruct((g, n, d), BF16),
        compiler_params=pltpu.CompilerParams(vmem_limit_bytes=VMEM_LIMIT_BYTES),
        name="keyfold",
    )(keys, wq_t)


def _candidate_layout():
    groups = []
    ra = row0 = 0
    while ra < TOPK:
        nb = TOPK // (ra + 1)
        if nb > 1:
            kind, valid, step = "row", nb, 1
        else:
            kind, valid, step = "col", TOPK - ra, TOPK - ra
        rows = -(-valid // SUBLANES) * SUBLANES
        groups.append((kind, ra, valid, rows, row0))
        ra += step
        row0 += rows
    return groups


_CAND_GROUPS = _candidate_layout()
_BIG = float(1 << 20)


def _top16(s):
    nb, n = s.shape[0] // SUBLANES, s.shape[1]
    blocks = [s[SUBLANES * v:SUBLANES * (v + 1), :] for v in range(nb)]
    sub = lax.broadcasted_iota(jnp.int32, (SUBLANES, n), 0).astype(F32)
    vals, pos = [], []
    for _ in range(TOPK):
        best = blocks[0]
        first = jnp.zeros((SUBLANES, n), F32)
        for v in range(1, nb):
            first = jnp.where(blocks[v] > best, float(v), first)
            best = jnp.maximum(best, blocks[v])
        m = jnp.max(best, axis=0, keepdims=True)
        p = jnp.min(jnp.where(best == m, first * float(SUBLANES) + sub, _BIG), axis=0, keepdims=True)
        off = p - sub
        blocks = [jnp.where(off == float(SUBLANES * v), -jnp.inf, blocks[v]) for v in range(nb)]
        vals.append(m)
        pos.append(p)
    return jnp.concatenate(vals, axis=0), jnp.concatenate(pos, axis=0)


def _pick_rank(table, rk):
    out = jnp.zeros_like(table)
    for r in range(TOPK):
        out = jnp.where(rk == r, table[r:r + 1, :], out)
    return out


def _batcher_pairs(n):
    pairs = []
    p = 1
    while p < n:
        k = p
        while k >= 1:
            for j in range(k % p, n - k, 2 * k):
                for i in range(min(k, n - j - k)):
                    if (i + j) // (2 * p) == (i + j + k) // (2 * p):
                        pairs.append((i + j, i + j + k))
            k //= 2
        p *= 2
    return pairs


_SORT_PAIRS = _batcher_pairs(N_KEYS // SUBLANES)


def _merge_top(groups, n_out):
    out_v, out_i = [], []
    for r in range(n_out):
        heads = [(g[0][0], g[1][0]) for g in groups]
        top = heads[0][0]
        for hv, _ in heads[1:]:
            top = jnp.maximum(top, hv)
        m = jnp.max(top, axis=0, keepdims=True)
        cand = None
        for hv, hi in heads:
            c = jnp.where(hv == m, hi, _BIG)
            cand = c if cand is None else jnp.minimum(cand, c)
        p = jnp.min(cand, axis=0, keepdims=True)
        out_v.append(m)
        out_i.append(p)
        left = n_out - 1 - r
        if left == 0:
            break
        for vals, ids in groups:
            hit = ids[0] == p
            depth = len(vals)
            for d in range(min(depth - 1, left)):
                vals[d] = jnp.where(hit, vals[d + 1], vals[d])
                ids[d] = jnp.where(hit, ids[d + 1], ids[d])
            if depth - 1 < left:
                vals[depth - 1] = jnp.where(hit, -jnp.inf, vals[depth - 1])
    return jnp.concatenate(out_v, axis=0), jnp.concatenate(out_i, axis=0), out_v


def _top16_sorted(s):
    nb, n = s.shape[0] // SUBLANES, s.shape[1]
    assert nb == N_KEYS // SUBLANES
    sub = lax.broadcasted_iota(jnp.int32, (SUBLANES, n), 0).astype(F32)
    vals = [s[SUBLANES * v:SUBLANES * (v + 1), :] for v in range(nb)]
    ids = [sub + float(SUBLANES * v) for v in range(nb)]
    for i, j in _SORT_PAIRS:
        swap = vals[j] > vals[i]
        vals[i], vals[j] = jnp.maximum(vals[i], vals[j]), jnp.minimum(vals[i], vals[j])
        ids[i], ids[j] = jnp.where(swap, ids[j], ids[i]), jnp.where(swap, ids[i], ids[j])
    top_v, top_i, rows = _merge_top([[vals, ids]], TOPK + 1)
    ok = top_v[0:TOPK, :] > jnp.concatenate(rows[1:], axis=0)
    return top_v[0:TOPK, :], top_i[0:TOPK, :], ok


def _route_head_fast(st):
    n = st.shape[1]
    s1, i1, ok1 = _top16_sorted(st[0:N_KEYS, :])
    s2, i2, ok2 = _top16_sorted(st[N_KEYS:2 * N_KEYS, :])
    sub = lax.broadcasted_iota(jnp.int32, (SUBLANES, n), 0)
    sub_f = sub.astype(F32)
    head = s1[0:1, :] + s2
    lo_v, lo_i = [head[0:SUBLANES, :]], [sub_f]
    for ra in range(1, TOPK):
        nb = TOPK // (ra + 1)
        v = s1[ra:ra + 1, :] + s2[0:SUBLANES, :]
        c = sub_f + float(ra * TOPK)
        if nb < SUBLANES:
            v = jnp.where(sub < nb, v, -jnp.inf)
            c = jnp.where(sub < nb, c, _BIG)
        lo_v.append(v)
        lo_i.append(c)
    hi = [[head[SUBLANES:TOPK, :]], [sub_f + float(SUBLANES)]]
    best_s, best_c, _ = _merge_top([[lo_v, lo_i], hi], TOPK)
    best_c = best_c.astype(jnp.int32)
    ra_sel = lax.shift_right_logical(best_c, 4).astype(F32)
    rb_sel = jnp.bitwise_and(best_c, TOPK - 1).astype(F32)
    e = jnp.exp(best_s - best_s[0:1, :])
    g = e / jnp.sum(e, axis=0, keepdims=True)
    ok = jnp.where(ok1, 0.0, 1.0) + jnp.where(ok2, 0.0, 1.0)
    return _pick_rank(i1, ra_sel), _pick_rank(i2, rb_sel), g, jnp.max(ok) == 0.0


def _route_head(st):
    n = st.shape[1]
    s1, i1 = _top16(st[0:N_KEYS, :])
    s2, i2 = _top16(st[N_KEYS:2 * N_KEYS, :])
    sub = lax.broadcasted_iota(jnp.int32, (SUBLANES, n), 0)
    parts = []
    for kind, ra, valid, rows, _ in _CAND_GROUPS:
        if kind == "row":
            part = s1[ra:ra + 1, :] + s2[0:rows, :]
        else:
            part = s1[ra:ra + rows, :] + s2[0:1, :]
        if valid < rows:
            assert rows == SUBLANES
            part = jnp.where(sub < valid, part, -jnp.inf)
        parts.append(part)
    best_s, best_p = _top16(jnp.concatenate(parts, axis=0))
    ra_sel = jnp.zeros_like(best_p)
    rb_sel = jnp.zeros_like(best_p)
    for kind, ra, valid, rows, row0 in _CAND_GROUPS:
        in_group = best_p >= float(row0)
        local = best_p - float(row0)
        if kind == "row":
            ra_sel = jnp.where(in_group, float(ra), ra_sel)
            rb_sel = jnp.where(in_group, local, rb_sel)
        else:
            ra_sel = jnp.where(in_group, local + float(ra), ra_sel)
            rb_sel = jnp.where(in_group, 0.0, rb_sel)
    e = jnp.exp(best_s - best_s[0:1, :])
    g = e / jnp.sum(e, axis=0, keepdims=True)
    return _pick_rank(i1, ra_sel), _pick_rank(i2, rb_sel), g


def _route_kernel(h_ref, ws_ref, a_ref, b_ref, g_ref):
    st = _dot_nt(ws_ref[...], h_ref[...])
    a_all, b_all, g_all = [], [], []
    for hd in range(N_HEADS):
        a, b, g = _route_head(st[hd * 2 * N_KEYS:(hd + 1) * 2 * N_KEYS, :])
        a_all.append(a)
        b_all.append(b)
        g_all.append(g)
    a_ref[...] = jnp.concatenate(a_all, axis=0).T.astype(jnp.int32)
    b_ref[...] = jnp.concatenate(b_all, axis=0).T.astype(jnp.int32)
    g_ref[...] = jnp.concatenate(g_all, axis=0).T


def _route(h2, ws):
    t, d = h2.shape
    tt = min(ROUTE_LANES, t)
    assert t % tt == 0
    hk = N_HEADS * TOPK
    out = jax.ShapeDtypeStruct((t, hk), jnp.int32)
    return pl.pallas_call(
        _route_kernel,
        grid=(t // tt,),
        in_specs=[pl.BlockSpec((tt, d), lambda i: (i, 0)),
                  pl.BlockSpec(ws.shape, lambda i: (0, 0))],
        out_specs=[pl.BlockSpec((tt, hk), lambda i: (i, 0))] * 3,
        out_shape=[out, out, jax.ShapeDtypeStruct((t, hk), F32)],
        compiler_params=pltpu.CompilerParams(
            dimension_semantics=("arbitrary",), vmem_limit_bytes=VMEM_LIMIT_BYTES),
        name="route",
    )(h2, ws)


def _experts_kernel(h_ref, hn_ref, ws_ref, a0_ref, b0_ref, g0_ref, u_ref, v_ref,
                    x1_ref, ada_ref, fg_ref, o_ref,
                    w3_ref, acc_ref, coef_ref, a_ref, b_ref, g_ref, at_ref, bt_ref, gt_ref,
                    *, tt, ec, n_chunks, n_tiles, final_norm):
    s = pl.program_id(0)
    i = s // n_chunks
    j = s % n_chunks
    rows_per_step = ec // N_KEYS
    half = tt // 2

    @pl.when(s == 0)
    def _():
        a_ref[...] = a0_ref[...]
        b_ref[...] = b0_ref[...]
        g_ref[...] = g0_ref[...]
        acc_ref[...] = jnp.zeros_like(acc_ref)
        coef_ref[...] = jnp.zeros_like(coef_ref)

    @pl.when((j == 0) & (i < n_tiles))
    def _():
        sub = lax.broadcasted_iota(jnp.int32, (N_KEYS, a_ref.shape[1]), 0)

        def build(t, carry):
            arow = a_ref[pl.ds(t, 1), :]
            brow = b_ref[pl.ds(t, 1), :]
            grow = 0.5 * g_ref[pl.ds(t, 1), :]
            pt = jnp.where(sub == arow, grow, 0.0).astype(BF16)
            qt = jnp.where(sub == brow, 1.0, 0.0).astype(BF16)
            wt = _dot_nt(pt, qt).astype(BF16)
            off = pl.multiple_of(t * W_PITCH, SUBLANES)
            w3_ref[pl.ds(off, N_KEYS // 2), :] = pltpu.bitcast(wt, jnp.uint32)
            return carry

        lax.fori_loop(0, tt, build, 0, unroll=BUILD_UNROLL)

    upc = 2 * N_HEADS // n_chunks
    tok_half = (j * upc) // N_HEADS
    hd0 = (j * upc) % N_HEADS
    hn_rows = pl.ds(pl.multiple_of(tok_half * half, half), half)
    scores = _dot_nt(ws_ref[...], hn_ref[hn_rows, :])

    acc_ref[...] += _dot(coef_ref[(s + 1) % 2], v_ref[...])

    act = _dot_nt(h_ref[...], u_ref[...])
    gel = act * (1.0 + lax.erf(act * np.float32(1.0 / np.sqrt(2.0))))
    parts = []
    for r in range(rows_per_step // 2):
        word = w3_ref[pl.ds(j * (rows_per_step // 2) + r, tt, stride=W_PITCH), :]
        w_even = lax.bitcast_convert_type(lax.shift_left(word, jnp.uint32(16)), F32)
        w_odd = lax.bitcast_convert_type(jnp.bitwise_and(word, jnp.uint32(0xFFFF0000)), F32)
        k = 2 * r
        parts.append((w_even * gel[:, k * N_KEYS:(k + 1) * N_KEYS]).astype(BF16))
        parts.append((w_odd * gel[:, (k + 1) * N_KEYS:(k + 2) * N_KEYS]).astype(BF16))
    coef_ref[s % 2] = jnp.concatenate(parts, axis=1)

    all_ok = None
    for q in range(upc):
        ra, rb, rg, fast_ok = _route_head_fast(scores[q * 2 * N_KEYS:(q + 1) * 2 * N_KEYS, :])
        at_ref[tok_half, hd0 + q] = ra
        bt_ref[tok_half, hd0 + q] = rb
        gt_ref[tok_half, hd0 + q] = rg
        all_ok = fast_ok if all_ok is None else jnp.logical_and(all_ok, fast_ok)

    @pl.when(jnp.logical_not(all_ok))
    def _():
        exact_scores = _dot_nt(ws_ref[...], hn_ref[hn_rows, :])
        for q in range(upc):
            ra, rb, rg = _route_head(exact_scores[q * 2 * N_KEYS:(q + 1) * 2 * N_KEYS, :])
            at_ref[tok_half, hd0 + q] = ra
            bt_ref[tok_half, hd0 + q] = rb
            gt_ref[tok_half, hd0 + q] = rg

    @pl.when((j == 0) & (s > 0))
    def _():
        g2 = ada_ref[0][5:6]
        x2 = x1_ref[...] + g2 * acc_ref[...]
        if final_norm:
            x2 = _rms(x2, fg_ref[...])
        o_ref[...] = x2
        acc_ref[...] = jnp.zeros_like(acc_ref)

    @pl.when(j == n_chunks - 1)
    def _():
        hk = N_HEADS * TOPK
        for hf in range(2):
            rows = slice(hf * half, (hf + 1) * half)
            a_ref[rows, :] = at_ref[hf].reshape(hk, half).T.astype(jnp.int32)
            b_ref[rows, :] = bt_ref[hf].reshape(hk, half).T.astype(jnp.int32)
            g_ref[rows, :] = gt_ref[hf].reshape(hk, half).T


def _experts(h2, ws, abg0, u, v, x1, ada, final_g, seq, final_norm):
    t, d = h2.shape
    ne = u.shape[0]
    hk = N_HEADS * TOPK
    tt = min(EXPERT_TOKENS, seq)
    ec = EXPERT_CHUNK
    half = tt // 2
    n_tiles = t // tt
    nc = ne // ec
    upc = 2 * N_HEADS // nc
    assert t % tt == 0 and seq % tt == 0 and ne == N_KEYS * N_KEYS and ne % ec == 0
    assert ec % (2 * N_KEYS) == 0 and upc * nc == 2 * N_HEADS and N_HEADS % upc == 0
    assert tt % BUILD_UNROLL == 0
    kern = functools.partial(_experts_kernel, tt=tt, ec=ec, n_chunks=nc, n_tiles=n_tiles,
                             final_norm=final_norm)
    tile = lambda s: jnp.minimum(s // nc, n_tiles - 1)
    done = lambda s: jnp.maximum(s - 1, 0) // nc
    first = lambda w: pl.BlockSpec((tt, w), lambda s: (0, 0), pipeline_mode=pl.Buffered(1))
    scr_t = pltpu.VMEM((2, N_HEADS, TOPK, half), F32)
    return pl.pallas_call(
        kern,
        grid=(n_tiles * nc + 1,),
        in_specs=[pl.BlockSpec((tt, d), lambda s: (tile(s), 0)),
                  pl.BlockSpec((tt, d), lambda s: (jnp.minimum(s // nc + 1, n_tiles - 1), 0)),
                  pl.BlockSpec((upc * 2 * N_KEYS, d), lambda s: (s % (N_HEADS // upc), 0)),
                  first(hk), first(hk), first(hk),
                  pl.BlockSpec((ec, d), lambda s: (s % nc, 0)),
                  pl.BlockSpec((ec, d), lambda s: ((s + nc - 1) % nc, 0)),
                  pl.BlockSpec((tt, d), lambda s: (done(s), 0)),
                  pl.BlockSpec((1,) + ada.shape[1:], lambda s: ((done(s) * tt) // seq, 0, 0)),
                  pl.BlockSpec(final_g.shape, lambda s: (0, 0))],
        out_specs=pl.BlockSpec((tt, d), lambda s: (done(s), 0)),
        out_shape=jax.ShapeDtypeStruct((t, d), F32),
        scratch_shapes=[pltpu.VMEM((tt * W_PITCH, N_KEYS), jnp.uint32),
                        pltpu.VMEM((tt, d), F32),
                        pltpu.VMEM((2, tt, ec), BF16),
                        pltpu.VMEM((tt, hk), jnp.int32),
                        pltpu.VMEM((tt, hk), jnp.int32),
                        pltpu.VMEM((tt, hk), F32),
                        scr_t, scr_t, scr_t],
        compiler_params=pltpu.CompilerParams(
            dimension_semantics=("arbitrary",),
            vmem_limit_bytes=VMEM_LIMIT_BYTES),
        name="experts",
    )(h2, h2, ws, *abg0, u, v, x1, ada, final_g)


def _block_diag(w):
    nh, hd, _ = w.shape
    eye = jnp.eye(nh, dtype=w.dtype)
    return (eye[:, None, :, None] * w[:, :, None, :]).reshape(nh * hd, nh * hd)


def kernel(x, c, w_ada, b_ada, norm1_g, w_in, conv_a_w, conv_b_w, conv_b_b, w_r, b_r, w_i, b_i,
           lru_lambda, gn_a, gn_b, w_out, norm2_g, w_q, sub_keys, expert_u, expert_v, final_g):
    bsz, seq, d = x.shape
    depth = w_ada.shape[0]
    cw = conv_a_w.shape[2]
    hd = cw // N_HEADS
    head_mean = _block_diag(jnp.full((N_HEADS, hd, hd), 1.0 / hd, F32)).astype(BF16)
    row = lambda p: p.reshape(1, -1)
    for l in range(depth):
        ada = _adaln(c, w_ada[l], b_ada[l]).reshape(bsz, 6, d)
        w_gate = jnp.concatenate([_block_diag(w_r[l]), _block_diag(w_i[l])], axis=1).astype(BF16)
        b_gate = jnp.concatenate([row(b_r[l]), row(b_i[l])], axis=1)
        x1, h2 = _mixer(x, ada, row(norm1_g[l]), w_in[l].astype(BF16), conv_a_w[l], conv_b_w[l],
                        row(conv_b_b[l]), w_gate, b_gate, row(lru_lambda[l]), row(gn_a[l]),
                        row(gn_b[l]), head_mean, w_out[l].astype(BF16), row(norm2_g[l]))
        dk = sub_keys.shape[-1]
        keys = sub_keys[l].reshape(N_HEADS * 2, N_KEYS, dk)
        wq_t = w_q[l].T.reshape(N_HEADS * 2, dk, d)
        ws = _keyfold(keys, wq_t).reshape(N_HEADS * 2 * N_KEYS, d)
        h2f = h2.reshape(bsz * seq, d)
        abg0 = _route(h2f[:min(EXPERT_TOKENS, seq)], ws)
        x = _experts(h2f, ws, abg0, expert_u[l].astype(BF16), expert_v[l].astype(BF16),
                     x1.reshape(bsz * seq, d), ada, row(final_g), seq,
                     final_norm=(l == depth - 1)).reshape(bsz, seq, d)
    return x
```

```python
import functools

import numpy as np
import jax
import jax.numpy as jnp
from jax import lax
from jax.experimental import pallas as pl
from jax.experimental.pallas import tpu as pltpu

EPS = 1e-6
LRU_C = 8.0
N_HEADS = 8
TOPK = 16
N_KEYS = 128
SHORT_K = 3
LRU_K = 4
HALO = 8

SUBLANES = 8
LANES = 128
VMEM_LIMIT_BYTES = 56 * 1024 * 1024

W_PITCH = N_KEYS // 2 + SUBLANES
BUILD_UNROLL = 128
MIXER_ROWS = 512
EXPERT_TOKENS = 512
EXPERT_CHUNK = 1024
ROUTE_LANES = 256
RSQRT2 = float(1.0 / np.sqrt(2.0))
GELU_FOLD = RSQRT2

F32 = jnp.float32
BF16 = jnp.bfloat16
NT_DIMS = (((1,), (1,)), ((), ()))


def _dot(a, b):
    return jnp.dot(a, b, preferred_element_type=F32)


def _dot_nt(a, b, precision=None):
    return lax.dot_general(a, b, NT_DIMS, preferred_element_type=F32, precision=precision)


def _rms(x, g):
    return x * lax.rsqrt(jnp.mean(x * x, axis=-1, keepdims=True) + EPS) * g


def _adaln_kernel(c_ref, w_ref, b_ref, o_ref):
    c = c_ref[...]
    c_act = c * jax.nn.sigmoid(c)
    o_ref[...] = jnp.dot(c_act, w_ref[...], preferred_element_type=F32,
                         precision=lax.Precision.HIGHEST) + b_ref[...]


def _adaln(c, w, b):
    bsz, d = c.shape
    n = w.shape[1]
    blk = d
    return pl.pallas_call(
        _adaln_kernel,
        grid=(n // blk,),
        in_specs=[pl.BlockSpec((bsz, d), lambda j: (0, 0)),
                  pl.BlockSpec((d, blk), lambda j: (0, j)),
                  pl.BlockSpec((1, blk), lambda j: (0, j))],
        out_specs=pl.BlockSpec((bsz, blk), lambda j: (0, j)),
        out_shape=jax.ShapeDtypeStruct((bsz, n), F32),
        compiler_params=pltpu.CompilerParams(vmem_limit_bytes=VMEM_LIMIT_BYTES),
        name="adaln",
    )(c, w, b.reshape(1, n))


def _causal_conv(buf_ref, v, w_ref, ts, k_w):
    buf_ref[HALO:HALO + ts, :] = v
    out = v * w_ref[k_w - 1:k_w, :]
    for k in range(k_w - 1):
        shift = k_w - 1 - k
        out = out + buf_ref[HALO - shift:HALO - shift + ts, :] * w_ref[k:k + 1, :]
    buf_ref[0:HALO, :] = v[ts - HALO:ts, :]
    return out


def _head_ms(y, m_ref):
    return _dot((y * y).astype(BF16), m_ref[...])


def _mixer_kernel(x_ref, ada_ref, n1g_ref, win_ref, caw_ref, cbw_ref, cbb_ref, wg_ref, bg_ref,
                  lam_ref, gna_ref, gnb_ref, hm_ref, wout_ref, n2g_ref,
                  x1_ref, h2_ref, bufa_ref, bufb_ref, carry_ref, *, ts, cw, rows):
    @pl.when(pl.program_id(1) == 0)
    def _():
        bufa_ref[:, 0:HALO, :] = jnp.zeros((rows, HALO, cw), F32)
        bufb_ref[:, 0:HALO, :] = jnp.zeros((rows, HALO, cw), F32)
        carry_ref[...] = jnp.zeros_like(carry_ref)

    for c in range(rows):
        _mixer_chain(x_ref.at[c], ada_ref.at[c], n1g_ref, win_ref, caw_ref, cbw_ref, cbb_ref,
                     wg_ref, bg_ref, lam_ref, gna_ref, gnb_ref, hm_ref, wout_ref, n2g_ref,
                     x1_ref.at[c], h2_ref.at[c], bufa_ref.at[c], bufb_ref.at[c], carry_ref.at[c],
                     ts=ts, cw=cw)


def _mixer_chain(x_ref, ada_ref, n1g_ref, win_ref, caw_ref, cbw_ref, cbb_ref, wg_ref, bg_ref,
                 lam_ref, gna_ref, gnb_ref, hm_ref, wout_ref, n2g_ref,
                 x1_ref, h2_ref, bufa_ref, bufb_ref, carry_ref, *, ts, cw):
    x = x_ref[...]
    ada = ada_ref[...]
    sh1, sc1, g1 = ada[0:1], ada[1:2], ada[2:3]
    sh2, sc2 = ada[3:4], ada[4:5]

    h = _rms(x, n1g_ref[...]) * (1.0 + sc1) + sh1
    z = _dot(h.astype(BF16), win_ref[...])
    gate_b = z[:, 0:cw]
    gate_c = z[:, cw:2 * cw]
    xa = z[:, 2 * cw:3 * cw]
    xr = z[:, 3 * cw:4 * cw]
    gr = z[:, 4 * cw:5 * cw]

    y_a = gate_b * _causal_conv(bufa_ref, gate_c * xa, caw_ref, ts, SHORT_K)

    xc = _causal_conv(bufb_ref, xr, cbw_ref, ts, LRU_K) + cbb_ref[...]
    pre = _dot(xc.astype(BF16), wg_ref[...]) + bg_ref[...]
    r = jax.nn.sigmoid(pre[:, 0:cw])
    i = jax.nn.sigmoid(pre[:, cw:2 * cw])
    nl = -lam_ref[...]
    softplus = jnp.maximum(nl, 0.0) + jnp.log1p(jnp.exp(-jnp.abs(nl)))
    log_a = (-LRU_C) * r * softplus
    a = jnp.exp(log_a)
    u = jnp.sqrt(-jnp.tanh(log_a) * (a * a + 1.0)) * (i * xc)

    ng = ts // SUBLANES
    acc_a = a.reshape(ng, SUBLANES, cw)
    acc_b = u.reshape(ng, SUBLANES, cw)
    sub = lax.broadcasted_iota(jnp.int32, (ng, SUBLANES, cw), 1)
    d = 1
    while d < SUBLANES:
        keep = sub >= d
        a_sh = jnp.where(keep, pltpu.roll(acc_a, d, 1), 1.0)
        b_sh = jnp.where(keep, pltpu.roll(acc_b, d, 1), 0.0)
        acc_b = acc_a * b_sh + acc_b
        acc_a = acc_a * a_sh
        d *= 2
    state = carry_ref[0:1, :]
    groups = []
    for g in range(ng):
        hg = acc_a[g] * state + acc_b[g]
        groups.append(hg)
        state = hg[SUBLANES - 1:SUBLANES, :]
    hseq = jnp.concatenate(groups, axis=0)
    carry_ref[0:1, :] = state

    y_b = hseq * jax.nn.gelu(gr, approximate=True)

    na = y_a * lax.rsqrt(_head_ms(y_a, hm_ref) + EPS) * gna_ref[...]
    nb = y_b * lax.rsqrt(_head_ms(y_b, hm_ref) + EPS) * gnb_ref[...]
    y = _dot(na.astype(BF16), wout_ref[0:cw, :]) + _dot(nb.astype(BF16), wout_ref[cw:2 * cw, :])

    x1 = x + g1 * y
    x1_ref[...] = x1
    h2_ref[...] = (_rms(x1, n2g_ref[...]) * (1.0 + sc2) + sh2).astype(BF16)


def _mixer(x, ada, n1g, w_in, caw, cbw, cbb, w_gate, b_gate, lam, gna, gnb, head_mean, w_out, n2g):
    bsz, seq, d = x.shape
    cw = caw.shape[1]
    ts = min(MIXER_ROWS, seq)
    rows = 2 if bsz % 2 == 0 else 1
    assert seq % ts == 0 and ts % SUBLANES == 0 and ts >= HALO
    full = lambda a: pl.BlockSpec(a.shape, lambda b, s: (0,) * a.ndim)
    kern = functools.partial(_mixer_kernel, ts=ts, cw=cw, rows=rows)
    return pl.pallas_call(
        kern,
        grid=(bsz // rows, seq // ts),
        in_specs=[pl.BlockSpec((rows, ts, d), lambda b, s: (b, s, 0)),
                  pl.BlockSpec((rows,) + ada.shape[1:], lambda b, s: (b, 0, 0)),
                  full(n1g), full(w_in), full(caw), full(cbw), full(cbb), full(w_gate),
                  full(b_gate), full(lam), full(gna), full(gnb), full(head_mean), full(w_out),
                  full(n2g)],
        out_specs=[pl.BlockSpec((rows, ts, d), lambda b, s: (b, s, 0)),
                   pl.BlockSpec((rows, ts, d), lambda b, s: (b, s, 0))],
        out_shape=[jax.ShapeDtypeStruct((bsz, seq, d), F32),
                   jax.ShapeDtypeStruct((bsz, seq, d), BF16)],
        scratch_shapes=[pltpu.VMEM((rows, ts + HALO, cw), F32),
                        pltpu.VMEM((rows, ts + HALO, cw), F32),
                        pltpu.VMEM((rows, SUBLANES, cw), F32)],
        compiler_params=pltpu.CompilerParams(
            dimension_semantics=("arbitrary", "arbitrary"),
            vmem_limit_bytes=VMEM_LIMIT_BYTES),
        name="mixer",
    )(x, ada, n1g, w_in, caw, cbw, cbb, w_gate, b_gate, lam, gna, gnb, head_mean, w_out, n2g)


def _keyfold_kernel(k_ref, wqt_ref, o_ref):
    o_ref[0] = jnp.dot(k_ref[0], wqt_ref[0], preferred_element_type=F32,
                       precision=lax.Precision.HIGHEST).astype(BF16)


def _keyfold(keys, wq_t):
    g, n, dk = keys.shape
    d = wq_t.shape[2]
    return pl.pallas_call(
        _keyfold_kernel,
        grid=(g,),
        in_specs=[pl.BlockSpec((1, n, dk), lambda i: (i, 0, 0)),
                  pl.BlockSpec((1, dk, d), lambda i: (i, 0, 0))],
        out_specs=pl.BlockSpec((1, n, d), lambda i: (i, 0, 0)),
        out_shape=jax.ShapeDtypeStruct((g, n, d), BF16),
        compiler_params=pltpu.CompilerParams(vmem_limit_bytes=VMEM_LIMIT_BYTES),
        name="keyfold",
    )(keys, wq_t)


def _candidate_layout():
    groups = []
    ra = row0 = 0
    while ra < TOPK:
        nb = TOPK // (ra + 1)
        if nb > 1:
            kind, valid, step = "row", nb, 1
        else:
            kind, valid, step = "col", TOPK - ra, TOPK - ra
        rows = -(-valid // SUBLANES) * SUBLANES
        groups.append((kind, ra, valid, rows, row0))
        ra += step
        row0 += rows
    return groups


_CAND_GROUPS = _candidate_layout()
_BIG = float(1 << 20)


def _top16(s):
    nb, n = s.shape[0] // SUBLANES, s.shape[1]
    blocks = [s[SUBLANES * v:SUBLANES * (v + 1), :] for v in range(nb)]
    sub = lax.broadcasted_iota(jnp.int32, (SUBLANES, n), 0).astype(F32)
    vals, pos = [], []
    for _ in range(TOPK):
        best = blocks[0]
        first = jnp.zeros((SUBLANES, n), F32)
        for v in range(1, nb):
            first = jnp.where(blocks[v] > best, float(v), first)
            best = jnp.maximum(best, blocks[v])
        m = jnp.max(best, axis=0, keepdims=True)
        p = jnp.min(jnp.where(best == m, first * float(SUBLANES) + sub, _BIG), axis=0, keepdims=True)
        off = p - sub
        blocks = [jnp.where(off == float(SUBLANES * v), -jnp.inf, blocks[v]) for v in range(nb)]
        vals.append(m)
        pos.append(p)
    return jnp.concatenate(vals, axis=0), jnp.concatenate(pos, axis=0)


def _pick_rank(table, rk):
    out = jnp.zeros_like(table)
    for r in range(TOPK):
        out = jnp.where(rk == r, table[r:r + 1, :], out)
    return out


def _batcher_pairs(n):
    pairs = []
    p = 1
    while p < n:
        k = p
        while k >= 1:
            for j in range(k % p, n - k, 2 * k):
                for i in range(min(k, n - j - k)):
                    if (i + j) // (2 * p) == (i + j + k) // (2 * p):
                        pairs.append((i + j, i + j + k))
            k //= 2
        p *= 2
    return pairs


_SORT_PAIRS = _batcher_pairs(N_KEYS // SUBLANES)


def _merge_top(groups, n_out):
    out_v, out_i = [], []
    for r in range(n_out):
        heads = [(g[0][0], g[1][0]) for g in groups]
        top = heads[0][0]
        for hv, _ in heads[1:]:
            top = jnp.maximum(top, hv)
        m = jnp.max(top, axis=0, keepdims=True)
        cand = None
        for hv, hi in heads:
            c = jnp.where(hv == m, hi, _BIG)
            cand = c if cand is None else jnp.minimum(cand, c)
        p = jnp.min(cand, axis=0, keepdims=True)
        out_v.append(m)
        out_i.append(p)
        left = n_out - 1 - r
        if left == 0:
            break
        for vals, ids in groups:
            hit = ids[0] == p
            depth = len(vals)
            for d in range(min(depth - 1, left)):
                vals[d] = jnp.where(hit, vals[d + 1], vals[d])
                ids[d] = jnp.where(hit, ids[d + 1], ids[d])
            if depth - 1 < left:
                vals[depth - 1] = jnp.where(hit, -jnp.inf, vals[depth - 1])
    return jnp.concatenate(out_v, axis=0), jnp.concatenate(out_i, axis=0), out_v


def _top16_sorted(s):
    nb, n = s.shape[0] // SUBLANES, s.shape[1]
    assert nb == N_KEYS // SUBLANES
    sub = lax.broadcasted_iota(jnp.int32, (SUBLANES, n), 0).astype(F32)
    vals = [s[SUBLANES * v:SUBLANES * (v + 1), :] for v in range(nb)]
    ids = [sub + float(SUBLANES * v) for v in range(nb)]
    for i, j in _SORT_PAIRS:
        swap = vals[j] > vals[i]
        vals[i], vals[j] = jnp.maximum(vals[i], vals[j]), jnp.minimum(vals[i], vals[j])
        ids[i], ids[j] = jnp.where(swap, ids[j], ids[i]), jnp.where(swap, ids[i], ids[j])
    top_v, top_i, rows = _merge_top([[vals, ids]], TOPK + 1)
    ok = top_v[0:TOPK, :] > jnp.concatenate(rows[1:], axis=0)
    return top_v[0:TOPK, :], top_i[0:TOPK, :], ok


def _route_head_fast(st):
    n = st.shape[1]
    s1, i1, ok1 = _top16_sorted(st[0:N_KEYS, :])
    s2, i2, ok2 = _top16_sorted(st[N_KEYS:2 * N_KEYS, :])
    sub = lax.broadcasted_iota(jnp.int32, (SUBLANES, n), 0)
    sub_f = sub.astype(F32)
    head = s1[0:1, :] + s2
    lo_v, lo_i = [head[0:SUBLANES, :]], [sub_f]
    for ra in range(1, TOPK):
        nb = TOPK // (ra + 1)
        v = s1[ra:ra + 1, :] + s2[0:SUBLANES, :]
        c = sub_f + float(ra * TOPK)
        if nb < SUBLANES:
            v = jnp.where(sub < nb, v, -jnp.inf)
            c = jnp.where(sub < nb, c, _BIG)
        lo_v.append(v)
        lo_i.append(c)
    hi = [[head[SUBLANES:TOPK, :]], [sub_f + float(SUBLANES)]]
    best_s, best_c, _ = _merge_top([[lo_v, lo_i], hi], TOPK)
    best_c = best_c.astype(jnp.int32)
    ra_sel = lax.shift_right_logical(best_c, 4).astype(F32)
    rb_sel = jnp.bitwise_and(best_c, TOPK - 1).astype(F32)
    e = jnp.exp(best_s - best_s[0:1, :])
    g = e / jnp.sum(e, axis=0, keepdims=True)
    ok = jnp.where(ok1, 0.0, 1.0) + jnp.where(ok2, 0.0, 1.0)
    return _pick_rank(i1, ra_sel), _pick_rank(i2, rb_sel), g, jnp.max(ok) == 0.0


def _route_head(st):
    n = st.shape[1]
    s1, i1 = _top16(st[0:N_KEYS, :])
    s2, i2 = _top16(st[N_KEYS:2 * N_KEYS, :])
    sub = lax.broadcasted_iota(jnp.int32, (SUBLANES, n), 0)
    parts = []
    for kind, ra, valid, rows, _ in _CAND_GROUPS:
        if kind == "row":
            part = s1[ra:ra + 1, :] + s2[0:rows, :]
        else:
            part = s1[ra:ra + rows, :] + s2[0:1, :]
        if valid < rows:
            assert rows == SUBLANES
            part = jnp.where(sub < valid, part, -jnp.inf)
        parts.append(part)
    best_s, best_p = _top16(jnp.concatenate(parts, axis=0))
    ra_sel = jnp.zeros_like(best_p)
    rb_sel = jnp.zeros_like(best_p)
    for kind, ra, valid, rows, row0 in _CAND_GROUPS:
        in_group = best_p >= float(row0)
        local = best_p - float(row0)
        if kind == "row":
            ra_sel = jnp.where(in_group, float(ra), ra_sel)
            rb_sel = jnp.where(in_group, local, rb_sel)
        else:
            ra_sel = jnp.where(in_group, local + float(ra), ra_sel)
            rb_sel = jnp.where(in_group, 0.0, rb_sel)
    e = jnp.exp(best_s - best_s[0:1, :])
    g = e / jnp.sum(e, axis=0, keepdims=True)
    return _pick_rank(i1, ra_sel), _pick_rank(i2, rb_sel), g


def _route_kernel(h_ref, ws_ref, a_ref, b_ref, g_ref):
    st = _dot_nt(ws_ref[...], h_ref[...])
    a_all, b_all, g_all = [], [], []
    for hd in range(N_HEADS):
        a, b, g = _route_head(st[hd * 2 * N_KEYS:(hd + 1) * 2 * N_KEYS, :])
        a_all.append(a)
        b_all.append(b)
        g_all.append(g)
    a_ref[...] = jnp.concatenate(a_all, axis=0).T.astype(jnp.int32)
    b_ref[...] = jnp.concatenate(b_all, axis=0).T.astype(jnp.int32)
    g_ref[...] = jnp.concatenate(g_all, axis=0).T


def _route(h2, ws):
    t, d = h2.shape
    tt = min(ROUTE_LANES, t)
    assert t % tt == 0
    hk = N_HEADS * TOPK
    out = jax.ShapeDtypeStruct((t, hk), jnp.int32)
    return pl.pallas_call(
        _route_kernel,
        grid=(t // tt,),
        in_specs=[pl.BlockSpec((tt, d), lambda i: (i, 0)),
                  pl.BlockSpec(ws.shape, lambda i: (0, 0))],
        out_specs=[pl.BlockSpec((tt, hk), lambda i: (i, 0))] * 3,
        out_shape=[out, out, jax.ShapeDtypeStruct((t, hk), F32)],
        compiler_params=pltpu.CompilerParams(
            dimension_semantics=("arbitrary",), vmem_limit_bytes=VMEM_LIMIT_BYTES),
        name="route",
    )(h2, ws)


def _experts_kernel(h_ref, hn_ref, ws_ref, a0_ref, b0_ref, g0_ref, u_ref, v_ref,
                    x1_ref, ada_ref, fg_ref, o_ref,
                    w3_ref, acc_ref, coef_ref, a_ref, b_ref, g_ref, at_ref, bt_ref, gt_ref,
                    *, tt, ec, n_chunks, n_tiles, final_norm):
    s = pl.program_id(0)
    i = s // n_chunks
    j = s % n_chunks
    rows_per_step = ec // N_KEYS
    half = tt // 2

    @pl.when(s == 0)
    def _():
        a_ref[...] = a0_ref[...]
        b_ref[...] = b0_ref[...]
        g_ref[...] = g0_ref[...]
        acc_ref[...] = jnp.zeros_like(acc_ref)
        coef_ref[...] = jnp.zeros_like(coef_ref)

    @pl.when((j == 0) & (i < n_tiles))
    def _():
        sub = lax.broadcasted_iota(jnp.int32, (N_KEYS, a_ref.shape[1]), 0)

        def build(t, carry):
            arow = a_ref[pl.ds(t, 1), :]
            brow = b_ref[pl.ds(t, 1), :]
            grow = GELU_FOLD * g_ref[pl.ds(t, 1), :]
            pt = jnp.where(sub == arow, grow, 0.0).astype(BF16)
            qt = jnp.where(sub == brow, 1.0, 0.0).astype(BF16)
            wt = _dot_nt(pt, qt).astype(BF16)
            off = pl.multiple_of(t * W_PITCH, SUBLANES)
            w3_ref[pl.ds(off, N_KEYS // 2), :] = pltpu.bitcast(wt, jnp.uint32)
            return carry

        lax.fori_loop(0, tt, build, 0, unroll=BUILD_UNROLL)

    upc = 2 * N_HEADS // n_chunks
    tok_half = (j * upc) // N_HEADS
    hd0 = (j * upc) % N_HEADS
    hn_rows = pl.ds(pl.multiple_of(tok_half * half, half), half)
    scores = _dot_nt(ws_ref[...], hn_ref[hn_rows, :])

    acc_ref[...] += _dot(coef_ref[(s + 1) % 2], v_ref[...])

    act = _dot_nt(h_ref[...], u_ref[...])
    gel = act * (1.0 + lax.erf(act))
    parts = []
    for r in range(rows_per_step // 2):
        word = w3_ref[pl.ds(j * (rows_per_step // 2) + r, tt, stride=W_PITCH), :]
        w_even = lax.bitcast_convert_type(lax.shift_left(word, jnp.uint32(16)), F32)
        w_odd = lax.bitcast_convert_type(jnp.bitwise_and(word, jnp.uint32(0xFFFF0000)), F32)
        k = 2 * r
        parts.append((w_even * gel[:, k * N_KEYS:(k + 1) * N_KEYS]).astype(BF16))
        parts.append((w_odd * gel[:, (k + 1) * N_KEYS:(k + 2) * N_KEYS]).astype(BF16))
    coef_ref[s % 2] = jnp.concatenate(parts, axis=1)

    all_ok = None
    for q in range(upc):
        ra, rb, rg, fast_ok = _route_head_fast(scores[q * 2 * N_KEYS:(q + 1) * 2 * N_KEYS, :])
        at_ref[tok_half, hd0 + q] = ra
        bt_ref[tok_half, hd0 + q] = rb
        gt_ref[tok_half, hd0 + q] = rg
        all_ok = fast_ok if all_ok is None else jnp.logical_and(all_ok, fast_ok)

    @pl.when(jnp.logical_not(all_ok))
    def _():
        exact_scores = _dot_nt(ws_ref[...], hn_ref[hn_rows, :])
        for q in range(upc):
            ra, rb, rg = _route_head(exact_scores[q * 2 * N_KEYS:(q + 1) * 2 * N_KEYS, :])
            at_ref[tok_half, hd0 + q] = ra
            bt_ref[tok_half, hd0 + q] = rb
            gt_ref[tok_half, hd0 + q] = rg

    @pl.when((j == 0) & (s > 0))
    def _():
        g2 = ada_ref[0][5:6]
        x2 = x1_ref[...] + g2 * acc_ref[...]
        if final_norm:
            x2 = _rms(x2, fg_ref[...])
        o_ref[...] = x2
        acc_ref[...] = jnp.zeros_like(acc_ref)

    @pl.when(j == n_chunks - 1)
    def _():
        hk = N_HEADS * TOPK
        for hf in range(2):
            rows = slice(hf * half, (hf + 1) * half)
            a_ref[rows, :] = at_ref[hf].reshape(hk, half).T.astype(jnp.int32)
            b_ref[rows, :] = bt_ref[hf].reshape(hk, half).T.astype(jnp.int32)
            g_ref[rows, :] = gt_ref[hf].reshape(hk, half).T


def _experts(h2, ws, abg0, u, v, x1, ada, final_g, seq, final_norm):
    t, d = h2.shape
    ne = u.shape[0]
    hk = N_HEADS * TOPK
    tt = min(EXPERT_TOKENS, seq)
    ec = EXPERT_CHUNK
    half = tt // 2
    n_tiles = t // tt
    nc = ne // ec
    upc = 2 * N_HEADS // nc
    assert t % tt == 0 and seq % tt == 0 and ne == N_KEYS * N_KEYS and ne % ec == 0
    assert ec % (2 * N_KEYS) == 0 and upc * nc == 2 * N_HEADS and N_HEADS % upc == 0
    assert tt % BUILD_UNROLL == 0
    kern = functools.partial(_experts_kernel, tt=tt, ec=ec, n_chunks=nc, n_tiles=n_tiles,
                             final_norm=final_norm)
    tile = lambda s: jnp.minimum(s // nc, n_tiles - 1)
    done = lambda s: jnp.maximum(s - 1, 0) // nc
    first = lambda w: pl.BlockSpec((tt, w), lambda s: (0, 0), pipeline_mode=pl.Buffered(1))
    scr_t = pltpu.VMEM((2, N_HEADS, TOPK, half), F32)
    return pl.pallas_call(
        kern,
        grid=(n_tiles * nc + 1,),
        in_specs=[pl.BlockSpec((tt, d), lambda s: (tile(s), 0)),
                  pl.BlockSpec((tt, d), lambda s: (jnp.minimum(s // nc + 1, n_tiles - 1), 0)),
                  pl.BlockSpec((upc * 2 * N_KEYS, d), lambda s: (s % (N_HEADS // upc), 0)),
                  first(hk), first(hk), first(hk),
                  pl.BlockSpec((ec, d), lambda s: (s % nc, 0)),
                  pl.BlockSpec((ec, d), lambda s: ((s + nc - 1) % nc, 0)),
                  pl.BlockSpec((tt, d), lambda s: (done(s), 0)),
                  pl.BlockSpec((1,) + ada.shape[1:], lambda s: ((done(s) * tt) // seq, 0, 0)),
                  pl.BlockSpec(final_g.shape, lambda s: (0, 0))],
        out_specs=pl.BlockSpec((tt, d), lambda s: (done(s), 0)),
        out_shape=jax.ShapeDtypeStruct((t, d), F32),
        scratch_shapes=[pltpu.VMEM((tt * W_PITCH, N_KEYS), jnp.uint32),
                        pltpu.VMEM((tt, d), F32),
                        pltpu.VMEM((2, tt, ec), BF16),
                        pltpu.VMEM((tt, hk), jnp.int32),
                        pltpu.VMEM((tt, hk), jnp.int32),
                        pltpu.VMEM((tt, hk), F32),
                        scr_t, scr_t, scr_t],
        compiler_params=pltpu.CompilerParams(
            dimension_semantics=("arbitrary",),
            vmem_limit_bytes=VMEM_LIMIT_BYTES),
        name="experts",
    )(h2, h2, ws, *abg0, u, v, x1, ada, final_g)


def _block_diag(w):
    nh, hd, _ = w.shape
    eye = jnp.eye(nh, dtype=w.dtype)
    return (eye[:, None, :, None] * w[:, :, None, :]).reshape(nh * hd, nh * hd)


def kernel(x, c, w_ada, b_ada, norm1_g, w_in, conv_a_w, conv_b_w, conv_b_b, w_r, b_r, w_i, b_i,
           lru_lambda, gn_a, gn_b, w_out, norm2_g, w_q, sub_keys, expert_u, expert_v, final_g):
    bsz, seq, d = x.shape
    depth = w_ada.shape[0]
    cw = conv_a_w.shape[2]
    hd = cw // N_HEADS
    head_mean = _block_diag(jnp.full((N_HEADS, hd, hd), 1.0 / hd, F32)).astype(BF16)
    row = lambda p: p.reshape(1, -1)
    for l in range(depth):
        ada = _adaln(c, w_ada[l], b_ada[l]).reshape(bsz, 6, d)
        w_gate = jnp.concatenate([_block_diag(w_r[l]), _block_diag(w_i[l])], axis=1).astype(BF16)
        b_gate = jnp.concatenate([row(b_r[l]), row(b_i[l])], axis=1)
        x1, h2 = _mixer(x, ada, row(norm1_g[l]), w_in[l].astype(BF16), conv_a_w[l], conv_b_w[l],
                        row(conv_b_b[l]), w_gate, b_gate, row(lru_lambda[l]), row(gn_a[l]),
                        row(gn_b[l]), head_mean, w_out[l].astype(BF16), row(norm2_g[l]))
        dk = sub_keys.shape[-1]
        keys = sub_keys[l].reshape(N_HEADS * 2, N_KEYS, dk)
        wq_t = w_q[l].T.reshape(N_HEADS * 2, dk, d)
        ws = _keyfold(keys, wq_t).reshape(N_HEADS * 2 * N_KEYS, d)
        h2f = h2.reshape(bsz * seq, d)
        abg0 = _route(h2f[:min(EXPERT_TOKENS, seq)], ws)
        x = _experts(h2f, ws, abg0, (expert_u[l] * RSQRT2).astype(BF16), expert_v[l].astype(BF16),
                     x1.reshape(bsz * seq, d), ada, row(final_g), seq,
                     final_norm=(l == depth - 1)).reshape(bsz, seq, d)
    return x
```

```python
import functools

import numpy as np
import jax
import jax.numpy as jnp
from jax import lax
from jax.experimental import pallas as pl
from jax.experimental.pallas import tpu as pltpu

EPS = 1e-6
LRU_C = 8.0
N_HEADS = 8
TOPK = 16
N_KEYS = 128
SHORT_K = 3
LRU_K = 4
HALO = 8

SUBLANES = 8
LANES = 128
VMEM_LIMIT_BYTES = 56 * 1024 * 1024

W_PITCH = N_KEYS + SUBLANES // 2
BUILD_UNROLL = 128
MIXER_ROWS = 512
EXPERT_TOKENS = 512
EXPERT_CHUNK = 1024
ROUTE_LANES = 256
RSQRT2 = float(1.0 / np.sqrt(2.0))
GELU_FOLD = RSQRT2

F32 = jnp.float32
BF16 = jnp.bfloat16
NT_DIMS = (((1,), (1,)), ((), ()))


def _dot(a, b):
    return jnp.dot(a, b, preferred_element_type=F32)


def _dot_nt(a, b, precision=None):
    return lax.dot_general(a, b, NT_DIMS, preferred_element_type=F32, precision=precision)


def _rms(x, g):
    return x * lax.rsqrt(jnp.mean(x * x, axis=-1, keepdims=True) + EPS) * g


def _adaln_kernel(c_ref, w_ref, b_ref, o_ref):
    c = c_ref[...]
    c_act = c * jax.nn.sigmoid(c)
    o_ref[...] = jnp.dot(c_act, w_ref[...], preferred_element_type=F32,
                         precision=lax.Precision.HIGHEST) + b_ref[...]


def _adaln(c, w, b):
    bsz, d = c.shape
    n = w.shape[1]
    blk = d
    return pl.pallas_call(
        _adaln_kernel,
        grid=(n // blk,),
        in_specs=[pl.BlockSpec((bsz, d), lambda j: (0, 0)),
                  pl.BlockSpec((d, blk), lambda j: (0, j)),
                  pl.BlockSpec((1, blk), lambda j: (0, j))],
        out_specs=pl.BlockSpec((bsz, blk), lambda j: (0, j)),
        out_shape=jax.ShapeDtypeStruct((bsz, n), F32),
        compiler_params=pltpu.CompilerParams(vmem_limit_bytes=VMEM_LIMIT_BYTES),
        name="adaln",
    )(c, w, b.reshape(1, n))


def _causal_conv(buf_ref, v, w_ref, ts, k_w):
    buf_ref[HALO:HALO + ts, :] = v
    out = v * w_ref[k_w - 1:k_w, :]
    for k in range(k_w - 1):
        shift = k_w - 1 - k
        out = out + buf_ref[HALO - shift:HALO - shift + ts, :] * w_ref[k:k + 1, :]
    buf_ref[0:HALO, :] = v[ts - HALO:ts, :]
    return out


def _head_ms(y, m_ref):
    return _dot((y * y).astype(BF16), m_ref[...])


def _mixer_kernel(x_ref, ada_ref, n1g_ref, win_ref, caw_ref, cbw_ref, cbb_ref, wg_ref, bg_ref,
                  lam_ref, gna_ref, gnb_ref, hm_ref, wout_ref, n2g_ref,
                  x1_ref, h2_ref, bufa_ref, bufb_ref, carry_ref, *, ts, cw, rows):
    @pl.when(pl.program_id(1) == 0)
    def _():
        bufa_ref[:, 0:HALO, :] = jnp.zeros((rows, HALO, cw), F32)
        bufb_ref[:, 0:HALO, :] = jnp.zeros((rows, HALO, cw), F32)
        carry_ref[...] = jnp.zeros_like(carry_ref)

    for c in range(rows):
        _mixer_chain(x_ref.at[c], ada_ref.at[c], n1g_ref, win_ref, caw_ref, cbw_ref, cbb_ref,
                     wg_ref, bg_ref, lam_ref, gna_ref, gnb_ref, hm_ref, wout_ref, n2g_ref,
                     x1_ref.at[c], h2_ref.at[c], bufa_ref.at[c], bufb_ref.at[c], carry_ref.at[c],
                     ts=ts, cw=cw)


def _mixer_chain(x_ref, ada_ref, n1g_ref, win_ref, caw_ref, cbw_ref, cbb_ref, wg_ref, bg_ref,
                 lam_ref, gna_ref, gnb_ref, hm_ref, wout_ref, n2g_ref,
                 x1_ref, h2_ref, bufa_ref, bufb_ref, carry_ref, *, ts, cw):
    x = x_ref[...]
    ada = ada_ref[...]
    sh1, sc1, g1 = ada[0:1], ada[1:2], ada[2:3]
    sh2, sc2 = ada[3:4], ada[4:5]

    h = _rms(x, n1g_ref[...]) * (1.0 + sc1) + sh1
    z = _dot(h.astype(BF16), win_ref[...])
    gate_b = z[:, 0:cw]
    gate_c = z[:, cw:2 * cw]
    xa = z[:, 2 * cw:3 * cw]
    xr = z[:, 3 * cw:4 * cw]
    gr = z[:, 4 * cw:5 * cw]

    y_a = gate_b * _causal_conv(bufa_ref, gate_c * xa, caw_ref, ts, SHORT_K)

    xc = _causal_conv(bufb_ref, xr, cbw_ref, ts, LRU_K) + cbb_ref[...]
    pre = _dot(xc.astype(BF16), wg_ref[...]) + bg_ref[...]
    r = jax.nn.sigmoid(pre[:, 0:cw])
    i = jax.nn.sigmoid(pre[:, cw:2 * cw])
    nl = -lam_ref[...]
    softplus = jnp.maximum(nl, 0.0) + jnp.log1p(jnp.exp(-jnp.abs(nl)))
    log_a = (-LRU_C) * r * softplus
    a = jnp.exp(log_a)
    u = jnp.sqrt(-jnp.tanh(log_a) * (a * a + 1.0)) * (i * xc)

    ng = ts // SUBLANES
    acc_a = a.reshape(ng, SUBLANES, cw)
    acc_b = u.reshape(ng, SUBLANES, cw)
    sub = lax.broadcasted_iota(jnp.int32, (ng, SUBLANES, cw), 1)
    d = 1
    while d < SUBLANES:
        keep = sub >= d
        a_sh = jnp.where(keep, pltpu.roll(acc_a, d, 1), 1.0)
        b_sh = jnp.where(keep, pltpu.roll(acc_b, d, 1), 0.0)
        acc_b = acc_a * b_sh + acc_b
        acc_a = acc_a * a_sh
        d *= 2
    state = carry_ref[0:1, :]
    groups = []
    for g in range(ng):
        hg = acc_a[g] * state + acc_b[g]
        groups.append(hg)
        state = hg[SUBLANES - 1:SUBLANES, :]
    hseq = jnp.concatenate(groups, axis=0)
    carry_ref[0:1, :] = state

    y_b = hseq * jax.nn.gelu(gr, approximate=True)

    na = y_a * lax.rsqrt(_head_ms(y_a, hm_ref) + EPS) * gna_ref[...]
    nb = y_b * lax.rsqrt(_head_ms(y_b, hm_ref) + EPS) * gnb_ref[...]
    y = _dot(na.astype(BF16), wout_ref[0:cw, :]) + _dot(nb.astype(BF16), wout_ref[cw:2 * cw, :])

    x1 = x + g1 * y
    x1_ref[...] = x1
    h2_ref[...] = (_rms(x1, n2g_ref[...]) * (1.0 + sc2) + sh2).astype(BF16)


def _mixer(x, ada, n1g, w_in, caw, cbw, cbb, w_gate, b_gate, lam, gna, gnb, head_mean, w_out, n2g):
    bsz, seq, d = x.shape
    cw = caw.shape[1]
    ts = min(MIXER_ROWS, seq)
    rows = 2 if bsz % 2 == 0 else 1
    assert seq % ts == 0 and ts % SUBLANES == 0 and ts >= HALO
    full = lambda a: pl.BlockSpec(a.shape, lambda b, s: (0,) * a.ndim)
    kern = functools.partial(_mixer_kernel, ts=ts, cw=cw, rows=rows)
    return pl.pallas_call(
        kern,
        grid=(bsz // rows, seq // ts),
        in_specs=[pl.BlockSpec((rows, ts, d), lambda b, s: (b, s, 0)),
                  pl.BlockSpec((rows,) + ada.shape[1:], lambda b, s: (b, 0, 0)),
                  full(n1g), full(w_in), full(caw), full(cbw), full(cbb), full(w_gate),
                  full(b_gate), full(lam), full(gna), full(gnb), full(head_mean), full(w_out),
                  full(n2g)],
        out_specs=[pl.BlockSpec((rows, ts, d), lambda b, s: (b, s, 0)),
                   pl.BlockSpec((rows, ts, d), lambda b, s: (b, s, 0))],
        out_shape=[jax.ShapeDtypeStruct((bsz, seq, d), F32),
                   jax.ShapeDtypeStruct((bsz, seq, d), BF16)],
        scratch_shapes=[pltpu.VMEM((rows, ts + HALO, cw), F32),
                        pltpu.VMEM((rows, ts + HALO, cw), F32),
                        pltpu.VMEM((rows, SUBLANES, cw), F32)],
        compiler_params=pltpu.CompilerParams(
            dimension_semantics=("arbitrary", "arbitrary"),
            vmem_limit_bytes=VMEM_LIMIT_BYTES),
        name="mixer",
    )(x, ada, n1g, w_in, caw, cbw, cbb, w_gate, b_gate, lam, gna, gnb, head_mean, w_out, n2g)


def _keyfold_kernel(k_ref, wqt_ref, o_ref):
    o_ref[0] = jnp.dot(k_ref[0], wqt_ref[0], preferred_element_type=F32,
                       precision=lax.Precision.HIGHEST).astype(BF16)


def _keyfold(keys, wq_t):
    g, n, dk = keys.shape
    d = wq_t.shape[2]
    return pl.pallas_call(
        _keyfold_kernel,
        grid=(g,),
        in_specs=[pl.BlockSpec((1, n, dk), lambda i: (i, 0, 0)),
                  pl.BlockSpec((1, dk, d), lambda i: (i, 0, 0))],
        out_specs=pl.BlockSpec((1, n, d), lambda i: (i, 0, 0)),
        out_shape=jax.ShapeDtypeStruct((g, n, d), BF16),
        compiler_params=pltpu.CompilerParams(vmem_limit_bytes=VMEM_LIMIT_BYTES),
        name="keyfold",
    )(keys, wq_t)


def _candidate_layout():
    groups = []
    ra = row0 = 0
    while ra < TOPK:
        nb = TOPK // (ra + 1)
        if nb > 1:
            kind, valid, step = "row", nb, 1
        else:
            kind, valid, step = "col", TOPK - ra, TOPK - ra
        rows = -(-valid // SUBLANES) * SUBLANES
        groups.append((kind, ra, valid, rows, row0))
        ra += step
        row0 += rows
    return groups


_CAND_GROUPS = _candidate_layout()
_BIG = float(1 << 20)


def _top16(s):
    nb, n = s.shape[0] // SUBLANES, s.shape[1]
    blocks = [s[SUBLANES * v:SUBLANES * (v + 1), :] for v in range(nb)]
    sub = lax.broadcasted_iota(jnp.int32, (SUBLANES, n), 0).astype(F32)
    vals, pos = [], []
    for _ in range(TOPK):
        best = blocks[0]
        first = jnp.zeros((SUBLANES, n), F32)
        for v in range(1, nb):
            first = jnp.where(blocks[v] > best, float(v), first)
            best = jnp.maximum(best, blocks[v])
        m = jnp.max(best, axis=0, keepdims=True)
        p = jnp.min(jnp.where(best == m, first * float(SUBLANES) + sub, _BIG), axis=0, keepdims=True)
        off = p - sub
        blocks = [jnp.where(off == float(SUBLANES * v), -jnp.inf, blocks[v]) for v in range(nb)]
        vals.append(m)
        pos.append(p)
    return jnp.concatenate(vals, axis=0), jnp.concatenate(pos, axis=0)


def _pick_rank(table, rk):
    out = jnp.zeros_like(table)
    for r in range(TOPK):
        out = jnp.where(rk == r, table[r:r + 1, :], out)
    return out


def _batcher_pairs(n):
    pairs = []
    p = 1
    while p < n:
        k = p
        while k >= 1:
            for j in range(k % p, n - k, 2 * k):
                for i in range(min(k, n - j - k)):
                    if (i + j) // (2 * p) == (i + j + k) // (2 * p):
                        pairs.append((i + j, i + j + k))
            k //= 2
        p *= 2
    return pairs


_SORT_PAIRS = _batcher_pairs(N_KEYS // SUBLANES)


def _merge_top(groups, n_out):
    out_v, out_i = [], []
    for r in range(n_out):
        heads = [(g[0][0], g[1][0]) for g in groups]
        top = heads[0][0]
        for hv, _ in heads[1:]:
            top = jnp.maximum(top, hv)
        m = jnp.max(top, axis=0, keepdims=True)
        cand = None
        for hv, hi in heads:
            c = jnp.where(hv == m, hi, _BIG)
            cand = c if cand is None else jnp.minimum(cand, c)
        p = jnp.min(cand, axis=0, keepdims=True)
        out_v.append(m)
        out_i.append(p)
        left = n_out - 1 - r
        if left == 0:
            break
        for vals, ids in groups:
            hit = ids[0] == p
            depth = len(vals)
            for d in range(min(depth - 1, left)):
                vals[d] = jnp.where(hit, vals[d + 1], vals[d])
                ids[d] = jnp.where(hit, ids[d + 1], ids[d])
            if depth - 1 < left:
                vals[depth - 1] = jnp.where(hit, -jnp.inf, vals[depth - 1])
    return jnp.concatenate(out_v, axis=0), jnp.concatenate(out_i, axis=0), out_v


def _top16_sorted(s):
    nb, n = s.shape[0] // SUBLANES, s.shape[1]
    assert nb == N_KEYS // SUBLANES
    sub = lax.broadcasted_iota(jnp.int32, (SUBLANES, n), 0).astype(F32)
    vals = [s[SUBLANES * v:SUBLANES * (v + 1), :] for v in range(nb)]
    ids = [sub + float(SUBLANES * v) for v in range(nb)]
    for i, j in _SORT_PAIRS:
        swap = vals[j] > vals[i]
        vals[i], vals[j] = jnp.maximum(vals[i], vals[j]), jnp.minimum(vals[i], vals[j])
        ids[i], ids[j] = jnp.where(swap, ids[j], ids[i]), jnp.where(swap, ids[i], ids[j])
    top_v, top_i, rows = _merge_top([[vals, ids]], TOPK + 1)
    ok = top_v[0:TOPK, :] > jnp.concatenate(rows[1:], axis=0)
    return top_v[0:TOPK, :], top_i[0:TOPK, :], ok


def _route_head_fast(st):
    n = st.shape[1]
    s1, i1, ok1 = _top16_sorted(st[0:N_KEYS, :])
    s2, i2, ok2 = _top16_sorted(st[N_KEYS:2 * N_KEYS, :])
    sub = lax.broadcasted_iota(jnp.int32, (SUBLANES, n), 0)
    sub_f = sub.astype(F32)
    head = s1[0:1, :] + s2
    lo_v, lo_i = [head[0:SUBLANES, :]], [sub_f]
    for ra in range(1, TOPK):
        nb = TOPK // (ra + 1)
        v = s1[ra:ra + 1, :] + s2[0:SUBLANES, :]
        c = sub_f + float(ra * TOPK)
        if nb < SUBLANES:
            v = jnp.where(sub < nb, v, -jnp.inf)
            c = jnp.where(sub < nb, c, _BIG)
        lo_v.append(v)
        lo_i.append(c)
    hi = [[head[SUBLANES:TOPK, :]], [sub_f + float(SUBLANES)]]
    best_s, best_c, _ = _merge_top([[lo_v, lo_i], hi], TOPK)
    best_c = best_c.astype(jnp.int32)
    ra_sel = lax.shift_right_logical(best_c, 4).astype(F32)
    rb_sel = jnp.bitwise_and(best_c, TOPK - 1).astype(F32)
    e = jnp.exp(best_s - best_s[0:1, :])
    g = e / jnp.sum(e, axis=0, keepdims=True)
    ok = jnp.where(ok1, 0.0, 1.0) + jnp.where(ok2, 0.0, 1.0)
    return _pick_rank(i1, ra_sel), _pick_rank(i2, rb_sel), g, jnp.max(ok) == 0.0


def _route_head(st):
    n = st.shape[1]
    s1, i1 = _top16(st[0:N_KEYS, :])
    s2, i2 = _top16(st[N_KEYS:2 * N_KEYS, :])
    sub = lax.broadcasted_iota(jnp.int32, (SUBLANES, n), 0)
    parts = []
    for kind, ra, valid, rows, _ in _CAND_GROUPS:
        if kind == "row":
            part = s1[ra:ra + 1, :] + s2[0:rows, :]
        else:
            part = s1[ra:ra + rows, :] + s2[0:1, :]
        if valid < rows:
            assert rows == SUBLANES
            part = jnp.where(sub < valid, part, -jnp.inf)
        parts.append(part)
    best_s, best_p = _top16(jnp.concatenate(parts, axis=0))
    ra_sel = jnp.zeros_like(best_p)
    rb_sel = jnp.zeros_like(best_p)
    for kind, ra, valid, rows, row0 in _CAND_GROUPS:
        in_group = best_p >= float(row0)
        local = best_p - float(row0)
        if kind == "row":
            ra_sel = jnp.where(in_group, float(ra), ra_sel)
            rb_sel = jnp.where(in_group, local, rb_sel)
        else:
            ra_sel = jnp.where(in_group, local + float(ra), ra_sel)
            rb_sel = jnp.where(in_group, 0.0, rb_sel)
    e = jnp.exp(best_s - best_s[0:1, :])
    g = e / jnp.sum(e, axis=0, keepdims=True)
    return _pick_rank(i1, ra_sel), _pick_rank(i2, rb_sel), g


def _route_kernel(h_ref, ws_ref, a_ref, b_ref, g_ref):
    st = _dot_nt(ws_ref[...], h_ref[...])
    a_all, b_all, g_all = [], [], []
    for hd in range(N_HEADS):
        a, b, g = _route_head(st[hd * 2 * N_KEYS:(hd + 1) * 2 * N_KEYS, :])
        a_all.append(a)
        b_all.append(b)
        g_all.append(g)
    a_ref[...] = jnp.concatenate(a_all, axis=0).T.astype(jnp.int32)
    b_ref[...] = jnp.concatenate(b_all, axis=0).T.astype(jnp.int32)
    g_ref[...] = jnp.concatenate(g_all, axis=0).T


def _route(h2, ws):
    t, d = h2.shape
    tt = min(ROUTE_LANES, t)
    assert t % tt == 0
    hk = N_HEADS * TOPK
    out = jax.ShapeDtypeStruct((t, hk), jnp.int32)
    return pl.pallas_call(
        _route_kernel,
        grid=(t // tt,),
        in_specs=[pl.BlockSpec((tt, d), lambda i: (i, 0)),
                  pl.BlockSpec(ws.shape, lambda i: (0, 0))],
        out_specs=[pl.BlockSpec((tt, hk), lambda i: (i, 0))] * 3,
        out_shape=[out, out, jax.ShapeDtypeStruct((t, hk), F32)],
        compiler_params=pltpu.CompilerParams(
            dimension_semantics=("arbitrary",), vmem_limit_bytes=VMEM_LIMIT_BYTES),
        name="route",
    )(h2, ws)


def _experts_kernel(h_ref, hn_ref, ws_ref, a0_ref, b0_ref, g0_ref, u_ref, v_ref,
                    x1_ref, ada_ref, fg_ref, o_ref,
                    w3_ref, acc_ref, coef_ref, a_ref, b_ref, g_ref, at_ref, bt_ref, gt_ref,
                    *, tt, ec, n_chunks, n_tiles, final_norm):
    s = pl.program_id(0)
    i = s // n_chunks
    j = s % n_chunks
    rows_per_step = ec // N_KEYS
    half = tt // 2

    @pl.when(s == 0)
    def _():
        a_ref[...] = a0_ref[...]
        b_ref[...] = b0_ref[...]
        g_ref[...] = g0_ref[...]
        acc_ref[...] = jnp.zeros_like(acc_ref)
        coef_ref[...] = jnp.zeros_like(coef_ref)

    @pl.when((j == 0) & (i < n_tiles))
    def _():
        sub = lax.broadcasted_iota(jnp.int32, (N_KEYS, a_ref.shape[1]), 0)

        def token_weights(t):
            arow = a_ref[pl.ds(t, 1), :]
            brow = b_ref[pl.ds(t, 1), :]
            grow = GELU_FOLD * g_ref[pl.ds(t, 1), :]
            pt = jnp.where(sub == arow, grow, 0.0).astype(BF16)
            qt = jnp.where(sub == brow, 1.0, 0.0).astype(BF16)
            wt = _dot_nt(pt, qt).astype(BF16).astype(F32)
            return lax.bitcast_convert_type(wt, jnp.uint32)

        pairs_per_trip = BUILD_UNROLL // 2

        def build(trip, carry):
            base = pl.multiple_of(trip * (pairs_per_trip * W_PITCH), SUBLANES)
            for q in range(pairs_per_trip):
                p = trip * pairs_per_trip + q
                lo = lax.shift_right_logical(token_weights(2 * p), jnp.uint32(16))
                hi = jnp.bitwise_and(token_weights(2 * p + 1), jnp.uint32(0xFFFF0000))
                w3_ref[pl.ds(base + q * W_PITCH, N_KEYS), :] = jnp.bitwise_or(lo, hi)
            return carry

        lax.fori_loop(0, tt // BUILD_UNROLL, build, 0)

    upc = 2 * N_HEADS // n_chunks
    tok_half = (j * upc) // N_HEADS
    hd0 = (j * upc) % N_HEADS
    hn_rows = pl.ds(pl.multiple_of(tok_half * half, half), half)
    scores = _dot_nt(ws_ref[...], hn_ref[hn_rows, :])

    acc_ref[...] += _dot(coef_ref[...], v_ref[...])

    act = _dot_nt(h_ref[...], u_ref[...])
    gel = act * (1.0 + lax.erf(act))
    parts = []
    for k in range(rows_per_step):
        word = w3_ref[pl.ds(j * rows_per_step + k, tt // 2, stride=W_PITCH), :]
        w_k = pltpu.bitcast(word, BF16)
        parts.append(w_k * gel[:, k * N_KEYS:(k + 1) * N_KEYS].astype(BF16))
    coef_ref[...] = jnp.concatenate(parts, axis=1)

    all_ok = None
    for q in range(upc):
        ra, rb, rg, fast_ok = _route_head_fast(scores[q * 2 * N_KEYS:(q + 1) * 2 * N_KEYS, :])
        at_ref[tok_half, hd0 + q] = ra
        bt_ref[tok_half, hd0 + q] = rb
        gt_ref[tok_half, hd0 + q] = rg
        all_ok = fast_ok if all_ok is None else jnp.logical_and(all_ok, fast_ok)

    @pl.when(jnp.logical_not(all_ok))
    def _():
        exact_scores = _dot_nt(ws_ref[...], hn_ref[hn_rows, :])
        for q in range(upc):
            ra, rb, rg = _route_head(exact_scores[q * 2 * N_KEYS:(q + 1) * 2 * N_KEYS, :])
            at_ref[tok_half, hd0 + q] = ra
            bt_ref[tok_half, hd0 + q] = rb
            gt_ref[tok_half, hd0 + q] = rg

    @pl.when((j == 0) & (s > 0))
    def _():
        g2 = ada_ref[0][5:6]
        x2 = x1_ref[...] + g2 * acc_ref[...]
        if final_norm:
            x2 = _rms(x2, fg_ref[...])
        o_ref[...] = x2
        acc_ref[...] = jnp.zeros_like(acc_ref)

    @pl.when(j == n_chunks - 1)
    def _():
        hk = N_HEADS * TOPK
        for hf in range(2):
            rows = slice(hf * half, (hf + 1) * half)
            a_ref[rows, :] = at_ref[hf].reshape(hk, half).T.astype(jnp.int32)
            b_ref[rows, :] = bt_ref[hf].reshape(hk, half).T.astype(jnp.int32)
            g_ref[rows, :] = gt_ref[hf].reshape(hk, half).T


def _experts(h2, ws, abg0, u, v, x1, ada, final_g, seq, final_norm):
    t, d = h2.shape
    ne = u.shape[0]
    hk = N_HEADS * TOPK
    tt = min(EXPERT_TOKENS, seq)
    ec = EXPERT_CHUNK
    half = tt // 2
    n_tiles = t // tt
    nc = ne // ec
    upc = 2 * N_HEADS // nc
    assert t % tt == 0 and seq % tt == 0 and ne == N_KEYS * N_KEYS and ne % ec == 0
    assert ec % (2 * N_KEYS) == 0 and upc * nc == 2 * N_HEADS and N_HEADS % upc == 0
    assert tt % BUILD_UNROLL == 0
    kern = functools.partial(_experts_kernel, tt=tt, ec=ec, n_chunks=nc, n_tiles=n_tiles,
                             final_norm=final_norm)
    tile = lambda s: jnp.minimum(s // nc, n_tiles - 1)
    done = lambda s: jnp.maximum(s - 1, 0) // nc
    first = lambda w: pl.BlockSpec((tt, w), lambda s: (0, 0), pipeline_mode=pl.Buffered(1))
    scr_t = pltpu.VMEM((2, N_HEADS, TOPK, half), F32)
    return pl.pallas_call(
        kern,
        grid=(n_tiles * nc + 1,),
        in_specs=[pl.BlockSpec((tt, d), lambda s: (tile(s), 0)),
                  pl.BlockSpec((tt, d), lambda s: (jnp.minimum(s // nc + 1, n_tiles - 1), 0)),
                  pl.BlockSpec((upc * 2 * N_KEYS, d), lambda s: (s % (N_HEADS // upc), 0)),
                  first(hk), first(hk), first(hk),
                  pl.BlockSpec((ec, d), lambda s: (s % nc, 0)),
                  pl.BlockSpec((ec, d), lambda s: ((s + nc - 1) % nc, 0)),
                  pl.BlockSpec((tt, d), lambda s: (done(s), 0)),
                  pl.BlockSpec((1,) + ada.shape[1:], lambda s: ((done(s) * tt) // seq, 0, 0)),
                  pl.BlockSpec(final_g.shape, lambda s: (0, 0))],
        out_specs=pl.BlockSpec((tt, d), lambda s: (done(s), 0)),
        out_shape=jax.ShapeDtypeStruct((t, d), F32),
        scratch_shapes=[pltpu.VMEM((tt // 2 * W_PITCH, N_KEYS), jnp.uint32),
                        pltpu.VMEM((tt, d), F32),
                        pltpu.VMEM((tt, ec), BF16),
                        pltpu.VMEM((tt, hk), jnp.int32),
                        pltpu.VMEM((tt, hk), jnp.int32),
                        pltpu.VMEM((tt, hk), F32),
                        scr_t, scr_t, scr_t],
        compiler_params=pltpu.CompilerParams(
            dimension_semantics=("arbitrary",),
            vmem_limit_bytes=VMEM_LIMIT_BYTES),
        name="experts",
    )(h2, h2, ws, *abg0, u, v, x1, ada, final_g)


def _block_diag(w):
    nh, hd, _ = w.shape
    eye = jnp.eye(nh, dtype=w.dtype)
    return (eye[:, None, :, None] * w[:, :, None, :]).reshape(nh * hd, nh * hd)


def kernel(x, c, w_ada, b_ada, norm1_g, w_in, conv_a_w, conv_b_w, conv_b_b, w_r, b_r, w_i, b_i,
           lru_lambda, gn_a, gn_b, w_out, norm2_g, w_q, sub_keys, expert_u, expert_v, final_g):
    bsz, seq, d = x.shape
    depth = w_ada.shape[0]
    cw = conv_a_w.shape[2]
    hd = cw // N_HEADS
    head_mean = _block_diag(jnp.full((N_HEADS, hd, hd), 1.0 / hd, F32)).astype(BF16)
    row = lambda p: p.reshape(1, -1)
    for l in range(depth):
        ada = _adaln(c, w_ada[l], b_ada[l]).reshape(bsz, 6, d)
        w_gate = jnp.concatenate([_block_diag(w_r[l]), _block_diag(w_i[l])], axis=1).astype(BF16)
        b_gate = jnp.concatenate([row(b_r[l]), row(b_i[l])], axis=1)
        x1, h2 = _mixer(x, ada, row(norm1_g[l]), w_in[l].astype(BF16), conv_a_w[l], conv_b_w[l],
                        row(conv_b_b[l]), w_gate, b_gate, row(lru_lambda[l]), row(gn_a[l]),
                        row(gn_b[l]), head_mean, w_out[l].astype(BF16), row(norm2_g[l]))
        dk = sub_keys.shape[-1]
        keys = sub_keys[l].reshape(N_HEADS * 2, N_KEYS, dk)
        wq_t = w_q[l].T.reshape(N_HEADS * 2, dk, d)
        ws = _keyfold(keys, wq_t).reshape(N_HEADS * 2 * N_KEYS, d)
        h2f = h2.reshape(bsz * seq, d)
        abg0 = _route(h2f[:min(EXPERT_TOKENS, seq)], ws)
        x = _experts(h2f, ws, abg0, (expert_u[l] * RSQRT2).astype(BF16), expert_v[l].astype(BF16),
                     x1.reshape(bsz * seq, d), ada, row(final_g), seq,
                     final_norm=(l == depth - 1)).reshape(bsz, seq, d)
    return x
```

```python
import functools

import numpy as np
import jax
import jax.numpy as jnp
from jax import lax
from jax.experimental import pallas as pl
from jax.experimental.pallas import tpu as pltpu

EPS = 1e-6
LRU_C = 8.0
N_HEADS = 8
TOPK = 16
N_KEYS = 128
SHORT_K = 3
LRU_K = 4
HALO = 8

SUBLANES = 8
LANES = 128
VMEM_LIMIT_BYTES = 56 * 1024 * 1024

W_PITCH = N_KEYS + SUBLANES // 2
BUILD_UNROLL = 128
MIXER_ROWS = 512
EXPERT_TOKENS = 512
EXPERT_CHUNK = 1024
ROUTE_LANES = 256
RSQRT2 = float(1.0 / np.sqrt(2.0))
GELU_FOLD = RSQRT2

F32 = jnp.float32
BF16 = jnp.bfloat16
NT_DIMS = (((1,), (1,)), ((), ()))


def _dot(a, b):
    return jnp.dot(a, b, preferred_element_type=F32)


def _dot_nt(a, b, precision=None):
    return lax.dot_general(a, b, NT_DIMS, preferred_element_type=F32, precision=precision)


def _rms(x, g):
    return x * lax.rsqrt(jnp.mean(x * x, axis=-1, keepdims=True) + EPS) * g


def _adaln_kernel(c_ref, w_ref, b_ref, o_ref):
    c = c_ref[...]
    c_act = c * jax.nn.sigmoid(c)
    o_ref[...] = jnp.dot(c_act, w_ref[...], preferred_element_type=F32,
                         precision=lax.Precision.HIGHEST) + b_ref[...]


def _adaln(c, w, b):
    bsz, d = c.shape
    n = w.shape[1]
    blk = d
    return pl.pallas_call(
        _adaln_kernel,
        grid=(n // blk,),
        in_specs=[pl.BlockSpec((bsz, d), lambda j: (0, 0)),
                  pl.BlockSpec((d, blk), lambda j: (0, j)),
                  pl.BlockSpec((1, blk), lambda j: (0, j))],
        out_specs=pl.BlockSpec((bsz, blk), lambda j: (0, j)),
        out_shape=jax.ShapeDtypeStruct((bsz, n), F32),
        compiler_params=pltpu.CompilerParams(vmem_limit_bytes=VMEM_LIMIT_BYTES),
        name="adaln",
    )(c, w, b.reshape(1, n))


def _causal_conv(buf_ref, v, w_ref, ts, k_w):
    buf_ref[HALO:HALO + ts, :] = v
    out = v * w_ref[k_w - 1:k_w, :]
    for k in range(k_w - 1):
        shift = k_w - 1 - k
        out = out + buf_ref[HALO - shift:HALO - shift + ts, :] * w_ref[k:k + 1, :]
    buf_ref[0:HALO, :] = v[ts - HALO:ts, :]
    return out


def _head_ms(y, m_ref):
    return _dot((y * y).astype(BF16), m_ref[...])


def _mixer_kernel(x_ref, ada_ref, n1g_ref, win_ref, caw_ref, cbw_ref, cbb_ref, wg_ref, bg_ref,
                  lam_ref, gna_ref, gnb_ref, hm_ref, wout_ref, n2g_ref,
                  x1_ref, h2_ref, bufa_ref, bufb_ref, carry_ref, *, ts, cw, rows):
    @pl.when(pl.program_id(1) == 0)
    def _():
        bufa_ref[:, 0:HALO, :] = jnp.zeros((rows, HALO, cw), F32)
        bufb_ref[:, 0:HALO, :] = jnp.zeros((rows, HALO, cw), F32)
        carry_ref[...] = jnp.zeros_like(carry_ref)

    for c in range(rows):
        _mixer_chain(x_ref.at[c], ada_ref.at[c], n1g_ref, win_ref, caw_ref, cbw_ref, cbb_ref,
                     wg_ref, bg_ref, lam_ref, gna_ref, gnb_ref, hm_ref, wout_ref, n2g_ref,
                     x1_ref.at[c], h2_ref.at[c], bufa_ref.at[c], bufb_ref.at[c], carry_ref.at[c],
                     ts=ts, cw=cw)


def _mixer_chain(x_ref, ada_ref, n1g_ref, win_ref, caw_ref, cbw_ref, cbb_ref, wg_ref, bg_ref,
                 lam_ref, gna_ref, gnb_ref, hm_ref, wout_ref, n2g_ref,
                 x1_ref, h2_ref, bufa_ref, bufb_ref, carry_ref, *, ts, cw):
    x = x_ref[...]
    ada = ada_ref[...]
    sh1, sc1, g1 = ada[0:1], ada[1:2], ada[2:3]
    sh2, sc2 = ada[3:4], ada[4:5]

    h = _rms(x, n1g_ref[...]) * (1.0 + sc1) + sh1
    z = _dot(h.astype(BF16), win_ref[...])
    gate_b = z[:, 0:cw]
    gate_c = z[:, cw:2 * cw]
    xa = z[:, 2 * cw:3 * cw]
    xr = z[:, 3 * cw:4 * cw]
    gr = z[:, 4 * cw:5 * cw]

    y_a = gate_b * _causal_conv(bufa_ref, gate_c * xa, caw_ref, ts, SHORT_K)

    xc = _causal_conv(bufb_ref, xr, cbw_ref, ts, LRU_K) + cbb_ref[...]
    pre = _dot(xc.astype(BF16), wg_ref[...]) + bg_ref[...]
    r = jax.nn.sigmoid(pre[:, 0:cw])
    i = jax.nn.sigmoid(pre[:, cw:2 * cw])
    nl = -lam_ref[...]
    softplus = jnp.maximum(nl, 0.0) + jnp.log1p(jnp.exp(-jnp.abs(nl)))
    log_a = (-LRU_C) * r * softplus
    a = jnp.exp(log_a)
    u = jnp.sqrt(-jnp.tanh(log_a) * (a * a + 1.0)) * (i * xc)

    ng = ts // SUBLANES
    acc_a = a.reshape(ng, SUBLANES, cw)
    acc_b = u.reshape(ng, SUBLANES, cw)
    sub = lax.broadcasted_iota(jnp.int32, (ng, SUBLANES, cw), 1)
    d = 1
    while d < SUBLANES:
        keep = sub >= d
        a_sh = jnp.where(keep, pltpu.roll(acc_a, d, 1), 1.0)
        b_sh = jnp.where(keep, pltpu.roll(acc_b, d, 1), 0.0)
        acc_b = acc_a * b_sh + acc_b
        acc_a = acc_a * a_sh
        d *= 2
    state = carry_ref[0:1, :]
    groups = []
    for g in range(ng):
        hg = acc_a[g] * state + acc_b[g]
        groups.append(hg)
        state = hg[SUBLANES - 1:SUBLANES, :]
    hseq = jnp.concatenate(groups, axis=0)
    carry_ref[0:1, :] = state

    y_b = hseq * jax.nn.gelu(gr, approximate=True)

    na = y_a * lax.rsqrt(_head_ms(y_a, hm_ref) + EPS) * gna_ref[...]
    nb = y_b * lax.rsqrt(_head_ms(y_b, hm_ref) + EPS) * gnb_ref[...]
    y = _dot(na.astype(BF16), wout_ref[0:cw, :]) + _dot(nb.astype(BF16), wout_ref[cw:2 * cw, :])

    x1 = x + g1 * y
    x1_ref[...] = x1
    h2_ref[...] = (_rms(x1, n2g_ref[...]) * (1.0 + sc2) + sh2).astype(BF16)


def _mixer(x, ada, n1g, w_in, caw, cbw, cbb, w_gate, b_gate, lam, gna, gnb, head_mean, w_out, n2g):
    bsz, seq, d = x.shape
    cw = caw.shape[1]
    ts = min(MIXER_ROWS, seq)
    rows = 2 if bsz % 2 == 0 else 1
    assert seq % ts == 0 and ts % SUBLANES == 0 and ts >= HALO
    full = lambda a: pl.BlockSpec(a.shape, lambda b, s: (0,) * a.ndim)
    kern = functools.partial(_mixer_kernel, ts=ts, cw=cw, rows=rows)
    return pl.pallas_call(
        kern,
        grid=(bsz // rows, seq // ts),
        in_specs=[pl.BlockSpec((rows, ts, d), lambda b, s: (b, s, 0)),
                  pl.BlockSpec((rows,) + ada.shape[1:], lambda b, s: (b, 0, 0)),
                  full(n1g), full(w_in), full(caw), full(cbw), full(cbb), full(w_gate),
                  full(b_gate), full(lam), full(gna), full(gnb), full(head_mean), full(w_out),
                  full(n2g)],
        out_specs=[pl.BlockSpec((rows, ts, d), lambda b, s: (b, s, 0)),
                   pl.BlockSpec((rows, ts, d), lambda b, s: (b, s, 0))],
        out_shape=[jax.ShapeDtypeStruct((bsz, seq, d), F32),
                   jax.ShapeDtypeStruct((bsz, seq, d), BF16)],
        scratch_shapes=[pltpu.VMEM((rows, ts + HALO, cw), F32),
                        pltpu.VMEM((rows, ts + HALO, cw), F32),
                        pltpu.VMEM((rows, SUBLANES, cw), F32)],
        compiler_params=pltpu.CompilerParams(
            dimension_semantics=("arbitrary", "arbitrary"),
            vmem_limit_bytes=VMEM_LIMIT_BYTES),
        name="mixer",
    )(x, ada, n1g, w_in, caw, cbw, cbb, w_gate, b_gate, lam, gna, gnb, head_mean, w_out, n2g)


def _keyfold_kernel(k_ref, wqt_ref, o_ref):
    o_ref[0] = jnp.dot(k_ref[0], wqt_ref[0], preferred_element_type=F32,
                       precision=lax.Precision.HIGHEST).astype(BF16)


def _keyfold(keys, wq_t):
    g, n, dk = keys.shape
    d = wq_t.shape[2]
    return pl.pallas_call(
        _keyfold_kernel,
        grid=(g,),
        in_specs=[pl.BlockSpec((1, n, dk), lambda i: (i, 0, 0)),
                  pl.BlockSpec((1, dk, d), lambda i: (i, 0, 0))],
        out_specs=pl.BlockSpec((1, n, d), lambda i: (i, 0, 0)),
        out_shape=jax.ShapeDtypeStruct((g, n, d), BF16),
        compiler_params=pltpu.CompilerParams(vmem_limit_bytes=VMEM_LIMIT_BYTES),
        name="keyfold",
    )(keys, wq_t)


def _candidate_layout():
    groups = []
    ra = row0 = 0
    while ra < TOPK:
        nb = TOPK // (ra + 1)
        if nb > 1:
            kind, valid, step = "row", nb, 1
        else:
            kind, valid, step = "col", TOPK - ra, TOPK - ra
        rows = -(-valid // SUBLANES) * SUBLANES
        groups.append((kind, ra, valid, rows, row0))
        ra += step
        row0 += rows
    return groups


_CAND_GROUPS = _candidate_layout()
_BIG = float(1 << 20)


def _top16(s):
    nb, n = s.shape[0] // SUBLANES, s.shape[1]
    blocks = [s[SUBLANES * v:SUBLANES * (v + 1), :] for v in range(nb)]
    sub = lax.broadcasted_iota(jnp.int32, (SUBLANES, n), 0).astype(F32)
    vals, pos = [], []
    for _ in range(TOPK):
        best = blocks[0]
        first = jnp.zeros((SUBLANES, n), F32)
        for v in range(1, nb):
            first = jnp.where(blocks[v] > best, float(v), first)
            best = jnp.maximum(best, blocks[v])
        m = jnp.max(best, axis=0, keepdims=True)
        p = jnp.min(jnp.where(best == m, first * float(SUBLANES) + sub, _BIG), axis=0, keepdims=True)
        off = p - sub
        blocks = [jnp.where(off == float(SUBLANES * v), -jnp.inf, blocks[v]) for v in range(nb)]
        vals.append(m)
        pos.append(p)
    return jnp.concatenate(vals, axis=0), jnp.concatenate(pos, axis=0)


def _pick_rank(table, rk):
    out = jnp.zeros_like(table)
    for r in range(TOPK):
        out = jnp.where(rk == r, table[r:r + 1, :], out)
    return out


def _batcher_pairs(n):
    pairs = []
    p = 1
    while p < n:
        k = p
        while k >= 1:
            for j in range(k % p, n - k, 2 * k):
                for i in range(min(k, n - j - k)):
                    if (i + j) // (2 * p) == (i + j + k) // (2 * p):
                        pairs.append((i + j, i + j + k))
            k //= 2
        p *= 2
    return pairs


_SORT_PAIRS = _batcher_pairs(N_KEYS // SUBLANES)


def _merge_top(groups, n_out):
    out_v, out_i = [], []
    for r in range(n_out):
        heads = [(g[0][0], g[1][0]) for g in groups]
        top = heads[0][0]
        for hv, _ in heads[1:]:
            top = jnp.maximum(top, hv)
        m = jnp.max(top, axis=0, keepdims=True)
        cand = None
        for hv, hi in heads:
            c = jnp.where(hv == m, hi, _BIG)
            cand = c if cand is None else jnp.minimum(cand, c)
        p = jnp.min(cand, axis=0, keepdims=True)
        out_v.append(m)
        out_i.append(p)
        left = n_out - 1 - r
        if left == 0:
            break
        for vals, ids in groups:
            hit = ids[0] == p
            depth = len(vals)
            for d in range(min(depth - 1, left)):
                vals[d] = jnp.where(hit, vals[d + 1], vals[d])
                ids[d] = jnp.where(hit, ids[d + 1], ids[d])
            if depth - 1 < left:
                vals[depth - 1] = jnp.where(hit, -jnp.inf, vals[depth - 1])
    return jnp.concatenate(out_v, axis=0), jnp.concatenate(out_i, axis=0), out_v


def _top16_sorted(s):
    nb, n = s.shape[0] // SUBLANES, s.shape[1]
    assert nb == N_KEYS // SUBLANES
    sub = lax.broadcasted_iota(jnp.int32, (SUBLANES, n), 0).astype(F32)
    vals = [s[SUBLANES * v:SUBLANES * (v + 1), :] for v in range(nb)]
    ids = [sub + float(SUBLANES * v) for v in range(nb)]
    for i, j in _SORT_PAIRS:
        swap = vals[j] > vals[i]
        vals[i], vals[j] = jnp.maximum(vals[i], vals[j]), jnp.minimum(vals[i], vals[j])
        ids[i], ids[j] = jnp.where(swap, ids[j], ids[i]), jnp.where(swap, ids[i], ids[j])
    top_v, top_i, rows = _merge_top([[vals, ids]], TOPK + 1)
    ok = top_v[0:TOPK, :] > jnp.concatenate(rows[1:], axis=0)
    return top_v[0:TOPK, :], top_i[0:TOPK, :], ok


def _route_head_fast(st):
    n = st.shape[1]
    s1, i1, ok1 = _top16_sorted(st[0:N_KEYS, :])
    s2, i2, ok2 = _top16_sorted(st[N_KEYS:2 * N_KEYS, :])
    sub = lax.broadcasted_iota(jnp.int32, (SUBLANES, n), 0)
    sub_f = sub.astype(F32)
    head = s1[0:1, :] + s2
    lo_v, lo_i = [head[0:SUBLANES, :]], [sub_f]
    for ra in range(1, TOPK):
        nb = TOPK // (ra + 1)
        v = s1[ra:ra + 1, :] + s2[0:SUBLANES, :]
        c = sub_f + float(ra * TOPK)
        if nb < SUBLANES:
            v = jnp.where(sub < nb, v, -jnp.inf)
            c = jnp.where(sub < nb, c, _BIG)
        lo_v.append(v)
        lo_i.append(c)
    hi = [[head[SUBLANES:TOPK, :]], [sub_f + float(SUBLANES)]]
    best_s, best_c, _ = _merge_top([[lo_v, lo_i], hi], TOPK)
    best_c = best_c.astype(jnp.int32)
    ra_sel = lax.shift_right_logical(best_c, 4).astype(F32)
    rb_sel = jnp.bitwise_and(best_c, TOPK - 1).astype(F32)
    e = jnp.exp(best_s - best_s[0:1, :])
    g = e / jnp.sum(e, axis=0, keepdims=True)
    ok = jnp.where(ok1, 0.0, 1.0) + jnp.where(ok2, 0.0, 1.0)
    return _pick_rank(i1, ra_sel), _pick_rank(i2, rb_sel), g, jnp.max(ok) == 0.0


def _route_head(st):
    n = st.shape[1]
    s1, i1 = _top16(st[0:N_KEYS, :])
    s2, i2 = _top16(st[N_KEYS:2 * N_KEYS, :])
    sub = lax.broadcasted_iota(jnp.int32, (SUBLANES, n), 0)
    parts = []
    for kind, ra, valid, rows, _ in _CAND_GROUPS:
        if kind == "row":
            part = s1[ra:ra + 1, :] + s2[0:rows, :]
        else:
            part = s1[ra:ra + rows, :] + s2[0:1, :]
        if valid < rows:
            assert rows == SUBLANES
            part = jnp.where(sub < valid, part, -jnp.inf)
        parts.append(part)
    best_s, best_p = _top16(jnp.concatenate(parts, axis=0))
    ra_sel = jnp.zeros_like(best_p)
    rb_sel = jnp.zeros_like(best_p)
    for kind, ra, valid, rows, row0 in _CAND_GROUPS:
        in_group = best_p >= float(row0)
        local = best_p - float(row0)
        if kind == "row":
            ra_sel = jnp.where(in_group, float(ra), ra_sel)
            rb_sel = jnp.where(in_group, local, rb_sel)
        else:
            ra_sel = jnp.where(in_group, local + float(ra), ra_sel)
            rb_sel = jnp.where(in_group, 0.0, rb_sel)
    e = jnp.exp(best_s - best_s[0:1, :])
    g = e / jnp.sum(e, axis=0, keepdims=True)
    return _pick_rank(i1, ra_sel), _pick_rank(i2, rb_sel), g


def _route_kernel(h_ref, ws_ref, a_ref, b_ref, g_ref):
    st = _dot_nt(ws_ref[...], h_ref[...])
    a_all, b_all, g_all = [], [], []
    for hd in range(N_HEADS):
        a, b, g = _route_head(st[hd * 2 * N_KEYS:(hd + 1) * 2 * N_KEYS, :])
        a_all.append(a)
        b_all.append(b)
        g_all.append(g)
    a_ref[...] = jnp.concatenate(a_all, axis=0).T.astype(jnp.int32)
    b_ref[...] = jnp.concatenate(b_all, axis=0).T.astype(jnp.int32)
    g_ref[...] = jnp.concatenate(g_all, axis=0).T


def _route(h2, ws):
    t, d = h2.shape
    tt = min(ROUTE_LANES, t)
    assert t % tt == 0
    hk = N_HEADS * TOPK
    out = jax.ShapeDtypeStruct((t, hk), jnp.int32)
    return pl.pallas_call(
        _route_kernel,
        grid=(t // tt,),
        in_specs=[pl.BlockSpec((tt, d), lambda i: (i, 0)),
                  pl.BlockSpec(ws.shape, lambda i: (0, 0))],
        out_specs=[pl.BlockSpec((tt, hk), lambda i: (i, 0))] * 3,
        out_shape=[out, out, jax.ShapeDtypeStruct((t, hk), F32)],
        compiler_params=pltpu.CompilerParams(
            dimension_semantics=("arbitrary",), vmem_limit_bytes=VMEM_LIMIT_BYTES),
        name="route",
    )(h2, ws)


def _experts_kernel(h_ref, hn_ref, ws_ref, a0_ref, b0_ref, g0_ref, u_ref, v_ref,
                    x1_ref, ada_ref, fg_ref, o_ref,
                    w3_ref, acc_ref, coef_ref, a_ref, b_ref, g_ref, at_ref, bt_ref, gt_ref,
                    *, tt, ec, n_chunks, n_tiles, final_norm):
    s = pl.program_id(0)
    i = s // n_chunks
    j = s % n_chunks
    rows_per_step = ec // N_KEYS
    half = tt // 2

    @pl.when(s == 0)
    def _():
        a_ref[...] = a0_ref[...]
        b_ref[...] = b0_ref[...]
        g_ref[...] = g0_ref[...]
        acc_ref[...] = jnp.zeros_like(acc_ref)
        coef_ref[...] = jnp.zeros_like(coef_ref)

    @pl.when((j == 0) & (i < n_tiles))
    def _():
        sub = lax.broadcasted_iota(jnp.int32, (N_KEYS, a_ref.shape[1]), 0)

        def token_weights(t):
            arow = a_ref[pl.ds(t, 1), :]
            brow = b_ref[pl.ds(t, 1), :]
            grow = GELU_FOLD * g_ref[pl.ds(t, 1), :]
            pt = jnp.where(sub == arow, grow, 0.0).astype(BF16)
            qt = jnp.where(sub == brow, 1.0, 0.0).astype(BF16)
            wt = _dot_nt(pt, qt).astype(BF16).astype(F32)
            return lax.bitcast_convert_type(wt, jnp.uint32)

        pairs_per_trip = BUILD_UNROLL // 2

        def build(trip, carry):
            base = pl.multiple_of(trip * (pairs_per_trip * W_PITCH), SUBLANES)
            for q in range(pairs_per_trip):
                p = trip * pairs_per_trip + q
                lo = lax.shift_right_logical(token_weights(2 * p), jnp.uint32(16))
                hi = jnp.bitwise_and(token_weights(2 * p + 1), jnp.uint32(0xFFFF0000))
                w3_ref[pl.ds(base + q * W_PITCH, N_KEYS), :] = jnp.bitwise_or(lo, hi)
            return carry

        lax.fori_loop(0, tt // BUILD_UNROLL, build, 0)

    upc = 2 * N_HEADS // n_chunks
    assert upc == 1
    tok_half = (j * upc) // N_HEADS
    hd0 = (j * upc) % N_HEADS
    hn_rows = pl.ds(pl.multiple_of(tok_half * half, half), half)
    scores_a = _dot_nt(ws_ref[0:N_KEYS, :], hn_ref[hn_rows, :])

    acc_ref[...] += _dot(coef_ref[...], v_ref[...])

    scores_b = _dot_nt(ws_ref[N_KEYS:2 * N_KEYS, :], hn_ref[hn_rows, :])
    scores = jnp.concatenate([scores_a, scores_b], axis=0)

    act = _dot_nt(h_ref[...], u_ref[...])
    gel = act * (1.0 + lax.erf(act))
    parts = []
    for k in range(rows_per_step):
        word = w3_ref[pl.ds(j * rows_per_step + k, tt // 2, stride=W_PITCH), :]
        w_k = pltpu.bitcast(word, BF16)
        parts.append(w_k * gel[:, k * N_KEYS:(k + 1) * N_KEYS].astype(BF16))
    coef_ref[...] = jnp.concatenate(parts, axis=1)

    all_ok = None
    for q in range(upc):
        ra, rb, rg, fast_ok = _route_head_fast(scores[q * 2 * N_KEYS:(q + 1) * 2 * N_KEYS, :])
        at_ref[tok_half, hd0 + q] = ra
        bt_ref[tok_half, hd0 + q] = rb
        gt_ref[tok_half, hd0 + q] = rg
        all_ok = fast_ok if all_ok is None else jnp.logical_and(all_ok, fast_ok)

    @pl.when(jnp.logical_not(all_ok))
    def _():
        exact_scores = _dot_nt(ws_ref[...], hn_ref[hn_rows, :])
        for q in range(upc):
            ra, rb, rg = _route_head(exact_scores[q * 2 * N_KEYS:(q + 1) * 2 * N_KEYS, :])
            at_ref[tok_half, hd0 + q] = ra
            bt_ref[tok_half, hd0 + q] = rb
            gt_ref[tok_half, hd0 + q] = rg

    @pl.when((j == 0) & (s > 0))
    def _():
        g2 = ada_ref[0][5:6]
        x2 = x1_ref[...] + g2 * acc_ref[...]
        if final_norm:
            x2 = _rms(x2, fg_ref[...])
        o_ref[...] = x2
        acc_ref[...] = jnp.zeros_like(acc_ref)

    @pl.when(j == n_chunks - 1)
    def _():
        hk = N_HEADS * TOPK
        for hf in range(2):
            rows = slice(hf * half, (hf + 1) * half)
            a_ref[rows, :] = at_ref[hf].reshape(hk, half).T.astype(jnp.int32)
            b_ref[rows, :] = bt_ref[hf].reshape(hk, half).T.astype(jnp.int32)
            g_ref[rows, :] = gt_ref[hf].reshape(hk, half).T


def _experts(h2, ws, abg0, u, v, x1, ada, final_g, seq, final_norm):
    t, d = h2.shape
    ne = u.shape[0]
    hk = N_HEADS * TOPK
    tt = min(EXPERT_TOKENS, seq)
    ec = EXPERT_CHUNK
    half = tt // 2
    n_tiles = t // tt
    nc = ne // ec
    upc = 2 * N_HEADS // nc
    assert t % tt == 0 and seq % tt == 0 and ne == N_KEYS * N_KEYS and ne % ec == 0
    assert ec % (2 * N_KEYS) == 0 and upc * nc == 2 * N_HEADS and N_HEADS % upc == 0
    assert tt % BUILD_UNROLL == 0
    kern = functools.partial(_experts_kernel, tt=tt, ec=ec, n_chunks=nc, n_tiles=n_tiles,
                             final_norm=final_norm)
    tile = lambda s: jnp.minimum(s // nc, n_tiles - 1)
    done = lambda s: jnp.maximum(s - 1, 0) // nc
    first = lambda w: pl.BlockSpec((tt, w), lambda s: (0, 0), pipeline_mode=pl.Buffered(1))
    scr_t = pltpu.VMEM((2, N_HEADS, TOPK, half), F32)
    return pl.pallas_call(
        kern,
        grid=(n_tiles * nc + 1,),
        in_specs=[pl.BlockSpec((tt, d), lambda s: (tile(s), 0)),
                  pl.BlockSpec((tt, d), lambda s: (jnp.minimum(s // nc + 1, n_tiles - 1), 0)),
                  pl.BlockSpec((upc * 2 * N_KEYS, d), lambda s: (s % (N_HEADS // upc), 0)),
                  first(hk), first(hk), first(hk),
                  pl.BlockSpec((ec, d), lambda s: (s % nc, 0)),
                  pl.BlockSpec((ec, d), lambda s: ((s + nc - 1) % nc, 0)),
                  pl.BlockSpec((tt, d), lambda s: (done(s), 0)),
                  pl.BlockSpec((1,) + ada.shape[1:], lambda s: ((done(s) * tt) // seq, 0, 0)),
                  pl.BlockSpec(final_g.shape, lambda s: (0, 0))],
        out_specs=pl.BlockSpec((tt, d), lambda s: (done(s), 0)),
        out_shape=jax.ShapeDtypeStruct((t, d), F32),
        scratch_shapes=[pltpu.VMEM((tt // 2 * W_PITCH, N_KEYS), jnp.uint32),
                        pltpu.VMEM((tt, d), F32),
                        pltpu.VMEM((tt, ec), BF16),
                        pltpu.VMEM((tt, hk), jnp.int32),
                        pltpu.VMEM((tt, hk), jnp.int32),
                        pltpu.VMEM((tt, hk), F32),
                        scr_t, scr_t, scr_t],
        compiler_params=pltpu.CompilerParams(
            dimension_semantics=("arbitrary",),
            vmem_limit_bytes=VMEM_LIMIT_BYTES),
        name="experts",
    )(h2, h2, ws, *abg0, u, v, x1, ada, final_g)


def _block_diag(w):
    nh, hd, _ = w.shape
    eye = jnp.eye(nh, dtype=w.dtype)
    return (eye[:, None, :, None] * w[:, :, None, :]).reshape(nh * hd, nh * hd)


def kernel(x, c, w_ada, b_ada, norm1_g, w_in, conv_a_w, conv_b_w, conv_b_b, w_r, b_r, w_i, b_i,
           lru_lambda, gn_a, gn_b, w_out, norm2_g, w_q, sub_keys, expert_u, expert_v, final_g):
    bsz, seq, d = x.shape
    depth = w_ada.shape[0]
    cw = conv_a_w.shape[2]
    hd = cw // N_HEADS
    head_mean = _block_diag(jnp.full((N_HEADS, hd, hd), 1.0 / hd, F32)).astype(BF16)
    row = lambda p: p.reshape(1, -1)
    for l in range(depth):
        ada = _adaln(c, w_ada[l], b_ada[l]).reshape(bsz, 6, d)
        w_gate = jnp.concatenate([_block_diag(w_r[l]), _block_diag(w_i[l])], axis=1).astype(BF16)
        b_gate = jnp.concatenate([row(b_r[l]), row(b_i[l])], axis=1)
        x1, h2 = _mixer(x, ada, row(norm1_g[l]), w_in[l].astype(BF16), conv_a_w[l], conv_b_w[l],
                        row(conv_b_b[l]), w_gate, b_gate, row(lru_lambda[l]), row(gn_a[l]),
                        row(gn_b[l]), head_mean, w_out[l].astype(BF16), row(norm2_g[l]))
        dk = sub_keys.shape[-1]
        keys = sub_keys[l].reshape(N_HEADS * 2, N_KEYS, dk)
        wq_t = w_q[l].T.reshape(N_HEADS * 2, dk, d)
        ws = _keyfold(keys, wq_t).reshape(N_HEADS * 2 * N_KEYS, d)
        h2f = h2.reshape(bsz * seq, d)
        abg0 = _route(h2f[:min(EXPERT_TOKENS, seq)], ws)
        x = _experts(h2f, ws, abg0, (expert_u[l] * RSQRT2).astype(BF16), expert_v[l].astype(BF16),
                     x1.reshape(bsz * seq, d), ada, row(final_g), seq,
                     final_norm=(l == depth - 1)).reshape(bsz, seq, d)
    return x
```

```python
import functools

import numpy as np
import jax
import jax.numpy as jnp
from jax import lax
from jax.experimental import pallas as pl
from jax.experimental.pallas import tpu as pltpu

EPS = 1e-6
LRU_C = 8.0
N_HEADS = 8
TOPK = 16
N_KEYS = 128
SHORT_K = 3
LRU_K = 4
HALO = 8

SUBLANES = 8
LANES = 128
VMEM_LIMIT_BYTES = 56 * 1024 * 1024

W_PITCH = N_KEYS + SUBLANES // 2
BUILD_UNROLL = 128
MIXER_ROWS = 512
EXPERT_TOKENS = 512
EXPERT_CHUNK = 1024
ROUTE_LANES = 256
RSQRT2 = float(1.0 / np.sqrt(2.0))
GELU_FOLD = RSQRT2

F32 = jnp.float32
BF16 = jnp.bfloat16
NT_DIMS = (((1,), (1,)), ((), ()))


def _dot(a, b):
    return jnp.dot(a, b, preferred_element_type=F32)


def _dot_nt(a, b, precision=None):
    return lax.dot_general(a, b, NT_DIMS, preferred_element_type=F32, precision=precision)


def _rms(x, g):
    return x * lax.rsqrt(jnp.mean(x * x, axis=-1, keepdims=True) + EPS) * g


def _adaln_kernel(c_ref, w_ref, b_ref, o_ref):
    c = c_ref[...]
    c_act = c * jax.nn.sigmoid(c)
    o_ref[...] = jnp.dot(c_act, w_ref[...], preferred_element_type=F32,
                         precision=lax.Precision.HIGHEST) + b_ref[...]


def _adaln(c, w, b):
    bsz, d = c.shape
    n = w.shape[1]
    blk = d
    return pl.pallas_call(
        _adaln_kernel,
        grid=(n // blk,),
        in_specs=[pl.BlockSpec((bsz, d), lambda j: (0, 0)),
                  pl.BlockSpec((d, blk), lambda j: (0, j)),
                  pl.BlockSpec((1, blk), lambda j: (0, j))],
        out_specs=pl.BlockSpec((bsz, blk), lambda j: (0, j)),
        out_shape=jax.ShapeDtypeStruct((bsz, n), F32),
        compiler_params=pltpu.CompilerParams(vmem_limit_bytes=VMEM_LIMIT_BYTES),
        name="adaln",
    )(c, w, b.reshape(1, n))


def _causal_conv(buf_ref, v, w_ref, ts, k_w):
    buf_ref[HALO:HALO + ts, :] = v
    out = v * w_ref[k_w - 1:k_w, :]
    for k in range(k_w - 1):
        shift = k_w - 1 - k
        out = out + buf_ref[HALO - shift:HALO - shift + ts, :] * w_ref[k:k + 1, :]
    buf_ref[0:HALO, :] = v[ts - HALO:ts, :]
    return out


def _head_ms(y, m_ref):
    return _dot((y * y).astype(BF16), m_ref[...])


def _mixer_kernel(x_ref, ada_ref, n1g_ref, win_ref, caw_ref, cbw_ref, cbb_ref, wg_ref, bg_ref,
                  lam_ref, gna_ref, gnb_ref, hm_ref, wout_ref, n2g_ref,
                  x1_ref, h2_ref, bufa_ref, bufb_ref, carry_ref, *, ts, cw, rows):
    @pl.when(pl.program_id(1) == 0)
    def _():
        bufa_ref[:, 0:HALO, :] = jnp.zeros((rows, HALO, cw), F32)
        bufb_ref[:, 0:HALO, :] = jnp.zeros((rows, HALO, cw), F32)
        carry_ref[...] = jnp.zeros_like(carry_ref)

    for c in range(rows):
        _mixer_chain(x_ref.at[c], ada_ref.at[c], n1g_ref, win_ref, caw_ref, cbw_ref, cbb_ref,
                     wg_ref, bg_ref, lam_ref, gna_ref, gnb_ref, hm_ref, wout_ref, n2g_ref,
                     x1_ref.at[c], h2_ref.at[c], bufa_ref.at[c], bufb_ref.at[c], carry_ref.at[c],
                     ts=ts, cw=cw)


def _mixer_chain(x_ref, ada_ref, n1g_ref, win_ref, caw_ref, cbw_ref, cbb_ref, wg_ref, bg_ref,
                 lam_ref, gna_ref, gnb_ref, hm_ref, wout_ref, n2g_ref,
                 x1_ref, h2_ref, bufa_ref, bufb_ref, carry_ref, *, ts, cw):
    x = x_ref[...]
    ada = ada_ref[...]
    sh1, sc1, g1 = ada[0:1], ada[1:2], ada[2:3]
    sh2, sc2 = ada[3:4], ada[4:5]

    h = _rms(x, n1g_ref[...]) * (1.0 + sc1) + sh1
    z = _dot(h.astype(BF16), win_ref[...])
    gate_b = z[:, 0:cw]
    gate_c = z[:, cw:2 * cw]
    xa = z[:, 2 * cw:3 * cw]
    xr = z[:, 3 * cw:4 * cw]
    gr = z[:, 4 * cw:5 * cw]

    y_a = gate_b * _causal_conv(bufa_ref, gate_c * xa, caw_ref, ts, SHORT_K)

    xc = _causal_conv(bufb_ref, xr, cbw_ref, ts, LRU_K) + cbb_ref[...]
    pre = _dot(xc.astype(BF16), wg_ref[...]) + bg_ref[...]
    r = jax.nn.sigmoid(pre[:, 0:cw])
    i = jax.nn.sigmoid(pre[:, cw:2 * cw])
    nl = -lam_ref[...]
    softplus = jnp.maximum(nl, 0.0) + jnp.log1p(jnp.exp(-jnp.abs(nl)))
    log_a = (-LRU_C) * r * softplus
    a = jnp.exp(log_a)
    u = jnp.sqrt(-jnp.tanh(log_a) * (a * a + 1.0)) * (i * xc)

    ng = ts // SUBLANES
    acc_a = a.reshape(ng, SUBLANES, cw)
    acc_b = u.reshape(ng, SUBLANES, cw)
    sub = lax.broadcasted_iota(jnp.int32, (ng, SUBLANES, cw), 1)
    d = 1
    while d < SUBLANES:
        keep = sub >= d
        a_sh = jnp.where(keep, pltpu.roll(acc_a, d, 1), 1.0)
        b_sh = jnp.where(keep, pltpu.roll(acc_b, d, 1), 0.0)
        acc_b = acc_a * b_sh + acc_b
        acc_a = acc_a * a_sh
        d *= 2
    state = carry_ref[0:1, :]
    groups = []
    for g in range(ng):
        hg = acc_a[g] * state + acc_b[g]
        groups.append(hg)
        state = hg[SUBLANES - 1:SUBLANES, :]
    hseq = jnp.concatenate(groups, axis=0)
    carry_ref[0:1, :] = state

    y_b = hseq * jax.nn.gelu(gr, approximate=True)

    na = y_a * lax.rsqrt(_head_ms(y_a, hm_ref) + EPS) * gna_ref[...]
    nb = y_b * lax.rsqrt(_head_ms(y_b, hm_ref) + EPS) * gnb_ref[...]
    y = _dot(na.astype(BF16), wout_ref[0:cw, :]) + _dot(nb.astype(BF16), wout_ref[cw:2 * cw, :])

    x1 = x + g1 * y
    x1_ref[...] = x1
    h2_ref[...] = (_rms(x1, n2g_ref[...]) * (1.0 + sc2) + sh2).astype(BF16)


def _mixer(x, ada, n1g, w_in, caw, cbw, cbb, w_gate, b_gate, lam, gna, gnb, head_mean, w_out, n2g):
    bsz, seq, d = x.shape
    cw = caw.shape[1]
    ts = min(MIXER_ROWS, seq)
    rows = 2 if bsz % 2 == 0 else 1
    assert seq % ts == 0 and ts % SUBLANES == 0 and ts >= HALO
    full = lambda a: pl.BlockSpec(a.shape, lambda b, s: (0,) * a.ndim)
    kern = functools.partial(_mixer_kernel, ts=ts, cw=cw, rows=rows)
    return pl.pallas_call(
        kern,
        grid=(bsz // rows, seq // ts),
        in_specs=[pl.BlockSpec((rows, ts, d), lambda b, s: (b, s, 0)),
                  pl.BlockSpec((rows,) + ada.shape[1:], lambda b, s: (b, 0, 0)),
                  full(n1g), full(w_in), full(caw), full(cbw), full(cbb), full(w_gate),
                  full(b_gate), full(lam), full(gna), full(gnb), full(head_mean), full(w_out),
                  full(n2g)],
        out_specs=[pl.BlockSpec((rows, ts, d), lambda b, s: (b, s, 0)),
                   pl.BlockSpec((rows, ts, d), lambda b, s: (b, s, 0))],
        out_shape=[jax.ShapeDtypeStruct((bsz, seq, d), F32),
                   jax.ShapeDtypeStruct((bsz, seq, d), BF16)],
        scratch_shapes=[pltpu.VMEM((rows, ts + HALO, cw), F32),
                        pltpu.VMEM((rows, ts + HALO, cw), F32),
                        pltpu.VMEM((rows, SUBLANES, cw), F32)],
        compiler_params=pltpu.CompilerParams(
            dimension_semantics=("arbitrary", "arbitrary"),
            vmem_limit_bytes=VMEM_LIMIT_BYTES),
        name="mixer",
    )(x, ada, n1g, w_in, caw, cbw, cbb, w_gate, b_gate, lam, gna, gnb, head_mean, w_out, n2g)


def _keyfold_kernel(k_ref, wqt_ref, o_ref):
    o_ref[0] = jnp.dot(k_ref[0], wqt_ref[0], preferred_element_type=F32,
                       precision=lax.Precision.HIGHEST).astype(BF16)


def _keyfold(keys, wq_t):
    g, n, dk = keys.shape
    d = wq_t.shape[2]
    return pl.pallas_call(
        _keyfold_kernel,
        grid=(g,),
        in_specs=[pl.BlockSpec((1, n, dk), lambda i: (i, 0, 0)),
                  pl.BlockSpec((1, dk, d), lambda i: (i, 0, 0))],
        out_specs=pl.BlockSpec((1, n, d), lambda i: (i, 0, 0)),
        out_shape=jax.ShapeDtypeStruct((g, n, d), BF16),
        compiler_params=pltpu.CompilerParams(vmem_limit_bytes=VMEM_LIMIT_BYTES),
        name="keyfold",
    )(keys, wq_t)


def _candidate_layout():
    groups = []
    ra = row0 = 0
    while ra < TOPK:
        nb = TOPK // (ra + 1)
        if nb > 1:
            kind, valid, step = "row", nb, 1
        else:
            kind, valid, step = "col", TOPK - ra, TOPK - ra
        rows = -(-valid // SUBLANES) * SUBLANES
        groups.append((kind, ra, valid, rows, row0))
        ra += step
        row0 += rows
    return groups


_CAND_GROUPS = _candidate_layout()
_BIG = float(1 << 20)


def _top16(s):
    nb, n = s.shape[0] // SUBLANES, s.shape[1]
    blocks = [s[SUBLANES * v:SUBLANES * (v + 1), :] for v in range(nb)]
    sub = lax.broadcasted_iota(jnp.int32, (SUBLANES, n), 0).astype(F32)
    vals, pos = [], []
    for _ in range(TOPK):
        best = blocks[0]
        first = jnp.zeros((SUBLANES, n), F32)
        for v in range(1, nb):
            first = jnp.where(blocks[v] > best, float(v), first)
            best = jnp.maximum(best, blocks[v])
        m = jnp.max(best, axis=0, keepdims=True)
        p = jnp.min(jnp.where(best == m, first * float(SUBLANES) + sub, _BIG), axis=0, keepdims=True)
        off = p - sub
        blocks = [jnp.where(off == float(SUBLANES * v), -jnp.inf, blocks[v]) for v in range(nb)]
        vals.append(m)
        pos.append(p)
    return jnp.concatenate(vals, axis=0), jnp.concatenate(pos, axis=0)


def _pick_rank(table, rk):
    out = jnp.zeros_like(table)
    for r in range(TOPK):
        out = jnp.where(rk == r, table[r:r + 1, :], out)
    return out


def _batcher_pairs(n):
    pairs = []
    p = 1
    while p < n:
        k = p
        while k >= 1:
            for j in range(k % p, n - k, 2 * k):
                for i in range(min(k, n - j - k)):
                    if (i + j) // (2 * p) == (i + j + k) // (2 * p):
                        pairs.append((i + j, i + j + k))
            k //= 2
        p *= 2
    return pairs


_SORT_PAIRS = _batcher_pairs(N_KEYS // SUBLANES)


def _merge_top(groups, n_out):
    out_v, out_i = [], []
    for r in range(n_out):
        heads = [(g[0][0], g[1][0]) for g in groups]
        top = heads[0][0]
        for hv, _ in heads[1:]:
            top = jnp.maximum(top, hv)
        m = jnp.max(top, axis=0, keepdims=True)
        cand = None
        for hv, hi in heads:
            c = jnp.where(hv == m, hi, _BIG)
            cand = c if cand is None else jnp.minimum(cand, c)
        p = jnp.min(cand, axis=0, keepdims=True)
        out_v.append(m)
        out_i.append(p)
        left = n_out - 1 - r
        if left == 0:
            break
        for vals, ids in groups:
            hit = ids[0] == p
            depth = len(vals)
            for d in range(min(depth - 1, left)):
                vals[d] = jnp.where(hit, vals[d + 1], vals[d])
                ids[d] = jnp.where(hit, ids[d + 1], ids[d])
            if depth - 1 < left:
                vals[depth - 1] = jnp.where(hit, -jnp.inf, vals[depth - 1])
    return jnp.concatenate(out_v, axis=0), jnp.concatenate(out_i, axis=0), out_v


def _top16_sorted(s):
    nb, n = s.shape[0] // SUBLANES, s.shape[1]
    assert nb == N_KEYS // SUBLANES
    sub = lax.broadcasted_iota(jnp.int32, (SUBLANES, n), 0).astype(F32)
    vals = [s[SUBLANES * v:SUBLANES * (v + 1), :] for v in range(nb)]
    ids = [sub + float(SUBLANES * v) for v in range(nb)]
    for i, j in _SORT_PAIRS:
        swap = vals[j] > vals[i]
        vals[i], vals[j] = jnp.maximum(vals[i], vals[j]), jnp.minimum(vals[i], vals[j])
        ids[i], ids[j] = jnp.where(swap, ids[j], ids[i]), jnp.where(swap, ids[i], ids[j])
    top_v, top_i, rows = _merge_top([[vals, ids]], TOPK + 1)
    ok = top_v[0:TOPK, :] > jnp.concatenate(rows[1:], axis=0)
    return top_v[0:TOPK, :], top_i[0:TOPK, :], ok


def _route_head_fast(st):
    n = st.shape[1]
    s1, i1, ok1 = _top16_sorted(st[0:N_KEYS, :])
    s2, i2, ok2 = _top16_sorted(st[N_KEYS:2 * N_KEYS, :])
    sub = lax.broadcasted_iota(jnp.int32, (SUBLANES, n), 0)
    sub_f = sub.astype(F32)
    head = s1[0:1, :] + s2
    lo_v, lo_i = [head[0:SUBLANES, :]], [sub_f]
    for ra in range(1, TOPK):
        nb = TOPK // (ra + 1)
        v = s1[ra:ra + 1, :] + s2[0:SUBLANES, :]
        c = sub_f + float(ra * TOPK)
        if nb < SUBLANES:
            v = jnp.where(sub < nb, v, -jnp.inf)
            c = jnp.where(sub < nb, c, _BIG)
        lo_v.append(v)
        lo_i.append(c)
    hi = [[head[SUBLANES:TOPK, :]], [sub_f + float(SUBLANES)]]
    best_s, best_c, _ = _merge_top([[lo_v, lo_i], hi], TOPK)
    best_c = best_c.astype(jnp.int32)
    ra_sel = lax.shift_right_logical(best_c, 4).astype(F32)
    rb_sel = jnp.bitwise_and(best_c, TOPK - 1).astype(F32)
    e = jnp.exp(best_s - best_s[0:1, :])
    g = e / jnp.sum(e, axis=0, keepdims=True)
    ok = jnp.where(ok1, 0.0, 1.0) + jnp.where(ok2, 0.0, 1.0)
    return _pick_rank(i1, ra_sel), _pick_rank(i2, rb_sel), g, jnp.max(ok) == 0.0


def _route_head(st):
    n = st.shape[1]
    s1, i1 = _top16(st[0:N_KEYS, :])
    s2, i2 = _top16(st[N_KEYS:2 * N_KEYS, :])
    sub = lax.broadcasted_iota(jnp.int32, (SUBLANES, n), 0)
    parts = []
    for kind, ra, valid, rows, _ in _CAND_GROUPS:
        if kind == "row":
            part = s1[ra:ra + 1, :] + s2[0:rows, :]
        else:
            part = s1[ra:ra + rows, :] + s2[0:1, :]
        if valid < rows:
            assert rows == SUBLANES
            part = jnp.where(sub < valid, part, -jnp.inf)
        parts.append(part)
    best_s, best_p = _top16(jnp.concatenate(parts, axis=0))
    ra_sel = jnp.zeros_like(best_p)
    rb_sel = jnp.zeros_like(best_p)
    for kind, ra, valid, rows, row0 in _CAND_GROUPS:
        in_group = best_p >= float(row0)
        local = best_p - float(row0)
        if kind == "row":
            ra_sel = jnp.where(in_group, float(ra), ra_sel)
            rb_sel = jnp.where(in_group, local, rb_sel)
        else:
            ra_sel = jnp.where(in_group, local + float(ra), ra_sel)
            rb_sel = jnp.where(in_group, 0.0, rb_sel)
    e = jnp.exp(best_s - best_s[0:1, :])
    g = e / jnp.sum(e, axis=0, keepdims=True)
    return _pick_rank(i1, ra_sel), _pick_rank(i2, rb_sel), g


def _route_kernel(h_ref, ws_ref, a_ref, b_ref, g_ref):
    st = _dot_nt(ws_ref[...], h_ref[...])
    a_all, b_all, g_all = [], [], []
    for hd in range(N_HEADS):
        a, b, g = _route_head(st[hd * 2 * N_KEYS:(hd + 1) * 2 * N_KEYS, :])
        a_all.append(a)
        b_all.append(b)
        g_all.append(g)
    a_ref[...] = jnp.concatenate(a_all, axis=0).T.astype(jnp.int32)
    b_ref[...] = jnp.concatenate(b_all, axis=0).T.astype(jnp.int32)
    g_ref[...] = jnp.concatenate(g_all, axis=0).T


def _route(h2, ws):
    t, d = h2.shape
    tt = min(ROUTE_LANES, t)
    assert t % tt == 0
    hk = N_HEADS * TOPK
    out = jax.ShapeDtypeStruct((t, hk), jnp.int32)
    return pl.pallas_call(
        _route_kernel,
        grid=(t // tt,),
        in_specs=[pl.BlockSpec((tt, d), lambda i: (i, 0)),
                  pl.BlockSpec(ws.shape, lambda i: (0, 0))],
        out_specs=[pl.BlockSpec((tt, hk), lambda i: (i, 0))] * 3,
        out_shape=[out, out, jax.ShapeDtypeStruct((t, hk), F32)],
        compiler_params=pltpu.CompilerParams(
            dimension_semantics=("arbitrary",), vmem_limit_bytes=VMEM_LIMIT_BYTES),
        name="route",
    )(h2, ws)


def _experts_kernel(h_ref, hn_ref, ws_ref, a0_ref, b0_ref, g0_ref, u_ref, v_ref,
                    x1_ref, ada_ref, fg_ref, o_ref,
                    w3_ref, acc_ref, coef_ref, a_ref, b_ref, g_ref, at_ref, bt_ref, gt_ref,
                    *, tt, ec, n_chunks, n_tiles, final_norm):
    s = pl.program_id(0)
    i = s // n_chunks
    j = s % n_chunks
    rows_per_step = ec // N_KEYS
    half = tt // 2

    @pl.when(s == 0)
    def _():
        a_ref[...] = a0_ref[...]
        b_ref[...] = b0_ref[...]
        g_ref[...] = g0_ref[...]
        acc_ref[...] = jnp.zeros_like(acc_ref)
        coef_ref[...] = jnp.zeros_like(coef_ref)

    @pl.when((j == 0) & (i < n_tiles))
    def _():
        sub = lax.broadcasted_iota(jnp.int32, (N_KEYS, a_ref.shape[1]), 0)

        def token_weights(t):
            arow = a_ref[pl.ds(t, 1), :]
            brow = b_ref[pl.ds(t, 1), :]
            grow = GELU_FOLD * g_ref[pl.ds(t, 1), :]
            pt = jnp.where(sub == arow, grow, 0.0).astype(BF16)
            qt = jnp.where(sub == brow, 1.0, 0.0).astype(BF16)
            wt = _dot_nt(pt, qt).astype(BF16).astype(F32)
            return lax.bitcast_convert_type(wt, jnp.uint32)

        pairs_per_trip = BUILD_UNROLL // 2

        def build(trip, carry):
            base = pl.multiple_of(trip * (pairs_per_trip * W_PITCH), SUBLANES)
            for q in range(pairs_per_trip):
                p = trip * pairs_per_trip + q
                lo = lax.shift_right_logical(token_weights(2 * p), jnp.uint32(16))
                hi = jnp.bitwise_and(token_weights(2 * p + 1), jnp.uint32(0xFFFF0000))
                w3_ref[pl.ds(base + q * W_PITCH, N_KEYS), :] = jnp.bitwise_or(lo, hi)
            return carry

        lax.fori_loop(0, tt // BUILD_UNROLL, build, 0)

    upc = 2 * N_HEADS // n_chunks
    tok_half = (j * upc) // N_HEADS
    hd0 = (j * upc) % N_HEADS
    hn_rows = pl.ds(pl.multiple_of(tok_half * half, half), half)
    scores = _dot_nt(ws_ref[...], hn_ref[hn_rows, :])

    acc_ref[...] += _dot(coef_ref[...], v_ref[...])

    act = _dot_nt(h_ref[...], u_ref[...]).astype(BF16)
    gel = act * (1.0 + lax.erf(act))
    parts = []
    for k in range(rows_per_step):
        word = w3_ref[pl.ds(j * rows_per_step + k, tt // 2, stride=W_PITCH), :]
        w_k = pltpu.bitcast(word, BF16)
        parts.append(w_k * gel[:, k * N_KEYS:(k + 1) * N_KEYS])
    coef_ref[...] = jnp.concatenate(parts, axis=1)

    all_ok = None
    for q in range(upc):
        ra, rb, rg, fast_ok = _route_head_fast(scores[q * 2 * N_KEYS:(q + 1) * 2 * N_KEYS, :])
        at_ref[tok_half, hd0 + q] = ra
        bt_ref[tok_half, hd0 + q] = rb
        gt_ref[tok_half, hd0 + q] = rg
        all_ok = fast_ok if all_ok is None else jnp.logical_and(all_ok, fast_ok)

    @pl.when(jnp.logical_not(all_ok))
    def _():
        exact_scores = _dot_nt(ws_ref[...], hn_ref[hn_rows, :])
        for q in range(upc):
            ra, rb, rg = _route_head(exact_scores[q * 2 * N_KEYS:(q + 1) * 2 * N_KEYS, :])
            at_ref[tok_half, hd0 + q] = ra
            bt_ref[tok_half, hd0 + q] = rb
            gt_ref[tok_half, hd0 + q] = rg

    @pl.when((j == 0) & (s > 0))
    def _():
        g2 = ada_ref[0][5:6]
        x2 = x1_ref[...] + g2 * acc_ref[...]
        if final_norm:
            x2 = _rms(x2, fg_ref[...])
        o_ref[...] = x2
        acc_ref[...] = jnp.zeros_like(acc_ref)

    @pl.when(j == n_chunks - 1)
    def _():
        hk = N_HEADS * TOPK
        for hf in range(2):
            rows = slice(hf * half, (hf + 1) * half)
            a_ref[rows, :] = at_ref[hf].reshape(hk, half).T.astype(jnp.int32)
            b_ref[rows, :] = bt_ref[hf].reshape(hk, half).T.astype(jnp.int32)
            g_ref[rows, :] = gt_ref[hf].reshape(hk, half).T


def _experts(h2, ws, abg0, u, v, x1, ada, final_g, seq, final_norm):
    t, d = h2.shape
    ne = u.shape[0]
    hk = N_HEADS * TOPK
    tt = min(EXPERT_TOKENS, seq)
    ec = EXPERT_CHUNK
    half = tt // 2
    n_tiles = t // tt
    nc = ne // ec
    upc = 2 * N_HEADS // nc
    assert t % tt == 0 and seq % tt == 0 and ne == N_KEYS * N_KEYS and ne % ec == 0
    assert ec % (2 * N_KEYS) == 0 and upc * nc == 2 * N_HEADS and N_HEADS % upc == 0
    assert tt % BUILD_UNROLL == 0
    kern = functools.partial(_experts_kernel, tt=tt, ec=ec, n_chunks=nc, n_tiles=n_tiles,
                             final_norm=final_norm)
    tile = lambda s: jnp.minimum(s // nc, n_tiles - 1)
    done = lambda s: jnp.maximum(s - 1, 0) // nc
    first = lambda w: pl.BlockSpec((tt, w), lambda s: (0, 0), pipeline_mode=pl.Buffered(1))
    scr_t = pltpu.VMEM((2, N_HEADS, TOPK, half), F32)
    return pl.pallas_call(
        kern,
        grid=(n_tiles * nc + 1,),
        in_specs=[pl.BlockSpec((tt, d), lambda s: (tile(s), 0)),
                  pl.BlockSpec((tt, d), lambda s: (jnp.minimum(s // nc + 1, n_tiles - 1), 0)),
                  pl.BlockSpec((upc * 2 * N_KEYS, d), lambda s: (s % (N_HEADS // upc), 0)),
                  first(hk), first(hk), first(hk),
                  pl.BlockSpec((ec, d), lambda s: (s % nc, 0)),
                  pl.BlockSpec((ec, d), lambda s: ((s + nc - 1) % nc, 0)),
                  pl.BlockSpec((tt, d), lambda s: (done(s), 0)),
                  pl.BlockSpec((1,) + ada.shape[1:], lambda s: ((done(s) * tt) // seq, 0, 0)),
                  pl.BlockSpec(final_g.shape, lambda s: (0, 0))],
        out_specs=pl.BlockSpec((tt, d), lambda s: (done(s), 0)),
        out_shape=jax.ShapeDtypeStruct((t, d), F32),
        scratch_shapes=[pltpu.VMEM((tt // 2 * W_PITCH, N_KEYS), jnp.uint32),
                        pltpu.VMEM((tt, d), F32),
                        pltpu.VMEM((tt, ec), BF16),
                        pltpu.VMEM((tt, hk), jnp.int32),
                        pltpu.VMEM((tt, hk), jnp.int32),
                        pltpu.VMEM((tt, hk), F32),
                        scr_t, scr_t, scr_t],
        compiler_params=pltpu.CompilerParams(
            dimension_semantics=("arbitrary",),
            vmem_limit_bytes=VMEM_LIMIT_BYTES),
        name="experts",
    )(h2, h2, ws, *abg0, u, v, x1, ada, final_g)


def _block_diag(w):
    nh, hd, _ = w.shape
    eye = jnp.eye(nh, dtype=w.dtype)
    return (eye[:, None, :, None] * w[:, :, None, :]).reshape(nh * hd, nh * hd)


def kernel(x, c, w_ada, b_ada, norm1_g, w_in, conv_a_w, conv_b_w, conv_b_b, w_r, b_r, w_i, b_i,
           lru_lambda, gn_a, gn_b, w_out, norm2_g, w_q, sub_keys, expert_u, expert_v, final_g):
    bsz, seq, d = x.shape
    depth = w_ada.shape[0]
    cw = conv_a_w.shape[2]
    hd = cw // N_HEADS
    head_mean = _block_diag(jnp.full((N_HEADS, hd, hd), 1.0 / hd, F32)).astype(BF16)
    row = lambda p: p.reshape(1, -1)
    for l in range(depth):
        ada = _adaln(c, w_ada[l], b_ada[l]).reshape(bsz, 6, d)
        w_gate = jnp.concatenate([_block_diag(w_r[l]), _block_diag(w_i[l])], axis=1).astype(BF16)
        b_gate = jnp.concatenate([row(b_r[l]), row(b_i[l])], axis=1)
        x1, h2 = _mixer(x, ada, row(norm1_g[l]), w_in[l].astype(BF16), conv_a_w[l], conv_b_w[l],
                        row(conv_b_b[l]), w_gate, b_gate, row(lru_lambda[l]), row(gn_a[l]),
                        row(gn_b[l]), head_mean, w_out[l].astype(BF16), row(norm2_g[l]))
        dk = sub_keys.shape[-1]
        keys = sub_keys[l].reshape(N_HEADS * 2, N_KEYS, dk)
        wq_t = w_q[l].T.reshape(N_HEADS * 2, dk, d)
        ws = _keyfold(keys, wq_t).reshape(N_HEADS * 2 * N_KEYS, d)
        h2f = h2.reshape(bsz * seq, d)
        abg0 = _route(h2f[:min(EXPERT_TOKENS, seq)], ws)
        x = _experts(h2f, ws, abg0, (expert_u[l] * RSQRT2).astype(BF16), expert_v[l].astype(BF16),
                     x1.reshape(bsz * seq, d), ada, row(final_g), seq,
                     final_norm=(l == depth - 1)).reshape(bsz, seq, d)
    return x
```

```python
import functools

import numpy as np
import jax
import jax.numpy as jnp
from jax import lax
from jax.experimental import pallas as pl
from jax.experimental.pallas import tpu as pltpu

EPS = 1e-6
LRU_C = 8.0
N_HEADS = 8
TOPK = 16
N_KEYS = 128
SHORT_K = 3
LRU_K = 4
HALO = 8

SUBLANES = 8
LANES = 128
VMEM_LIMIT_BYTES = 56 * 1024 * 1024

W_PITCH = N_KEYS + SUBLANES // 2
STAGE_SLOTS = 8
BUILD_UNROLL = 128
MIXER_ROWS = 512
EXPERT_TOKENS = 512
EXPERT_CHUNK = 1024
ROUTE_LANES = 2 * LANES
RSQRT2 = float(1.0 / np.sqrt(2.0))
GELU_FOLD = RSQRT2

F32 = jnp.float32
BF16 = jnp.bfloat16
NT_DIMS = (((1,), (1,)), ((), ()))


def _dot(a, b):
    return jnp.dot(a, b, preferred_element_type=F32)


def _dot_nt(a, b):
    return lax.dot_general(a, b, NT_DIMS, preferred_element_type=F32)


def _rms(x, g):
    return x * lax.rsqrt(jnp.mean(x * x, axis=-1, keepdims=True) + EPS) * g


def _adaln_kernel(c_ref, w_ref, b_ref, o_ref):
    c = c_ref[...]
    c_act = c * jax.nn.sigmoid(c)
    o_ref[...] = jnp.dot(c_act, w_ref[...], preferred_element_type=F32,
                         precision=lax.Precision.HIGHEST) + b_ref[...]


def _adaln(c, w, b):
    bsz, d = c.shape
    n = w.shape[1]
    blk = d
    return pl.pallas_call(
        _adaln_kernel,
        grid=(n // blk,),
        in_specs=[pl.BlockSpec((bsz, d), lambda j: (0, 0)),
                  pl.BlockSpec((d, blk), lambda j: (0, j)),
                  pl.BlockSpec((1, blk), lambda j: (0, j))],
        out_specs=pl.BlockSpec((bsz, blk), lambda j: (0, j)),
        out_shape=jax.ShapeDtypeStruct((bsz, n), F32),
        compiler_params=pltpu.CompilerParams(vmem_limit_bytes=VMEM_LIMIT_BYTES),
        name="adaln",
    )(c, w, b.reshape(1, n))


def _causal_conv(buf_ref, v, w_ref, ts, k_w):
    buf_ref[HALO:HALO + ts, :] = v
    out = v * w_ref[k_w - 1:k_w, :]
    for k in range(k_w - 1):
        shift = k_w - 1 - k
        out = out + buf_ref[HALO - shift:HALO - shift + ts, :] * w_ref[k:k + 1, :]
    buf_ref[0:HALO, :] = v[ts - HALO:ts, :]
    return out


def _head_ms(y, m_ref):
    return _dot((y * y).astype(BF16), m_ref[...])


def _mixer_kernel(x_ref, ada_ref, n1g_ref, win_ref, caw_ref, cbw_ref, cbb_ref, wg_ref, bg_ref,
                  lam_ref, gna_ref, gnb_ref, hm_ref, wout_ref, n2g_ref,
                  x1_ref, h2_ref, bufa_ref, bufb_ref, carry_ref, *, ts, cw, rows):
    @pl.when(pl.program_id(1) == 0)
    def _():
        bufa_ref[:, 0:HALO, :] = jnp.zeros((rows, HALO, cw), F32)
        bufb_ref[:, 0:HALO, :] = jnp.zeros((rows, HALO, cw), F32)
        carry_ref[...] = jnp.zeros_like(carry_ref)

    for c in range(rows):
        _mixer_chain(x_ref.at[c], ada_ref.at[c], n1g_ref, win_ref, caw_ref, cbw_ref, cbb_ref,
                     wg_ref, bg_ref, lam_ref, gna_ref, gnb_ref, hm_ref, wout_ref, n2g_ref,
                     x1_ref.at[c], h2_ref.at[c], bufa_ref.at[c], bufb_ref.at[c], carry_ref.at[c],
                     ts=ts, cw=cw)


def _mixer_chain(x_ref, ada_ref, n1g_ref, win_ref, caw_ref, cbw_ref, cbb_ref, wg_ref, bg_ref,
                 lam_ref, gna_ref, gnb_ref, hm_ref, wout_ref, n2g_ref,
                 x1_ref, h2_ref, bufa_ref, bufb_ref, carry_ref, *, ts, cw):
    x = x_ref[...]
    ada = ada_ref[...]
    sh1, sc1, g1 = ada[0:1], ada[1:2], ada[2:3]
    sh2, sc2 = ada[3:4], ada[4:5]

    h = _rms(x, n1g_ref[...] * (1.0 + sc1)) + sh1
    z = _dot(h.astype(BF16), win_ref[...])
    gate_b = z[:, 0:cw]
    gate_c = z[:, cw:2 * cw]
    xa = z[:, 2 * cw:3 * cw]
    xr = z[:, 3 * cw:4 * cw]
    gr = z[:, 4 * cw:5 * cw]

    y_a = gate_b * _causal_conv(bufa_ref, gate_c * xa, caw_ref, ts, SHORT_K)

    xc = _causal_conv(bufb_ref, xr, cbw_ref, ts, LRU_K) + cbb_ref[...]
    pre = _dot(xc.astype(BF16), wg_ref[...]) + bg_ref[...]
    r = jax.nn.sigmoid(pre[:, 0:cw])
    i = jax.nn.sigmoid(pre[:, cw:2 * cw])
    nl = -lam_ref[...]
    softplus = jnp.maximum(nl, 0.0) + jnp.log1p(jnp.exp(-jnp.abs(nl)))
    log_a = (-LRU_C) * r * softplus
    a = jnp.exp(log_a)
    u = jnp.sqrt(-jnp.tanh(log_a) * (a * a + 1.0)) * (i * xc)

    ng = ts // SUBLANES
    acc_a = a.reshape(ng, SUBLANES, cw)
    acc_b = u.reshape(ng, SUBLANES, cw)
    sub = lax.broadcasted_iota(jnp.int32, (ng, SUBLANES, cw), 1)
    d = 1
    while d < SUBLANES:
        keep = sub >= d
        a_sh = jnp.where(keep, pltpu.roll(acc_a, d, 1), 1.0)
        b_sh = jnp.where(keep, pltpu.roll(acc_b, d, 1), 0.0)
        acc_b = acc_a * b_sh + acc_b
        acc_a = acc_a * a_sh
        d *= 2
    state = carry_ref[0:1, :]
    groups = []
    for g in range(ng):
        hg = acc_a[g] * state + acc_b[g]
        groups.append(hg)
        state = hg[SUBLANES - 1:SUBLANES, :]
    hseq = jnp.concatenate(groups, axis=0)
    carry_ref[0:1, :] = state

    y_b = hseq * jax.nn.gelu(gr, approximate=True)

    na = y_a * lax.rsqrt(_head_ms(y_a, hm_ref) + EPS) * gna_ref[...]
    nb = y_b * lax.rsqrt(_head_ms(y_b, hm_ref) + EPS) * gnb_ref[...]
    y = _dot(na.astype(BF16), wout_ref[0:cw, :]) + _dot(nb.astype(BF16), wout_ref[cw:2 * cw, :])

    x1 = x + g1 * y
    x1_ref[...] = x1
    h2_ref[...] = (_rms(x1, n2g_ref[...] * (1.0 + sc2)) + sh2).astype(BF16)


def _mixer(x, ada, n1g, w_in, caw, cbw, cbb, w_gate, b_gate, lam, gna, gnb, head_mean, w_out, n2g):
    bsz, seq, d = x.shape
    cw = caw.shape[1]
    ts = min(MIXER_ROWS, seq)
    rows = 2 if bsz % 2 == 0 else 1
    assert seq % ts == 0 and ts % SUBLANES == 0 and ts >= HALO
    full = lambda a: pl.BlockSpec(a.shape, lambda b, s: (0,) * a.ndim)
    kern = functools.partial(_mixer_kernel, ts=ts, cw=cw, rows=rows)
    return pl.pallas_call(
        kern,
        grid=(bsz // rows, seq // ts),
        in_specs=[pl.BlockSpec((rows, ts, d), lambda b, s: (b, s, 0)),
                  pl.BlockSpec((rows,) + ada.shape[1:], lambda b, s: (b, 0, 0)),
                  full(n1g), full(w_in), full(caw), full(cbw), full(cbb), full(w_gate),
                  full(b_gate), full(lam), full(gna), full(gnb), full(head_mean), full(w_out),
                  full(n2g)],
        out_specs=[pl.BlockSpec((rows, ts, d), lambda b, s: (b, s, 0)),
                   pl.BlockSpec((rows, ts, d), lambda b, s: (b, s, 0))],
        out_shape=[jax.ShapeDtypeStruct((bsz, seq, d), F32),
                   jax.ShapeDtypeStruct((bsz, seq, d), BF16)],
        scratch_shapes=[pltpu.VMEM((rows, ts + HALO, cw), F32),
                        pltpu.VMEM((rows, ts + HALO, cw), F32),
                        pltpu.VMEM((rows, SUBLANES, cw), F32)],
        compiler_params=pltpu.CompilerParams(
            dimension_semantics=("arbitrary", "arbitrary"),
            vmem_limit_bytes=VMEM_LIMIT_BYTES),
        name="mixer",
    )(x, ada, n1g, w_in, caw, cbw, cbb, w_gate, b_gate, lam, gna, gnb, head_mean, w_out, n2g)


def _keyfold_kernel(k_ref, wqt_ref, o_ref):
    o_ref[0] = jnp.dot(k_ref[0], wqt_ref[0], preferred_element_type=F32,
                       precision=lax.Precision.HIGHEST).astype(BF16)


def _keyfold(keys, wq_t):
    g, n, dk = keys.shape
    d = wq_t.shape[2]
    return pl.pallas_call(
        _keyfold_kernel,
        grid=(g,),
        in_specs=[pl.BlockSpec((1, n, dk), lambda i: (i, 0, 0)),
                  pl.BlockSpec((1, dk, d), lambda i: (i, 0, 0))],
        out_specs=pl.BlockSpec((1, n, d), lambda i: (i, 0, 0)),
        out_shape=jax.ShapeDtypeStruct((g, n, d), BF16),
        compiler_params=pltpu.CompilerParams(vmem_limit_bytes=VMEM_LIMIT_BYTES),
        name="keyfold",
    )(keys, wq_t)


def _candidate_layout():
    groups = []
    ra = row0 = 0
    while ra < TOPK:
        nb = TOPK // (ra + 1)
        if nb > 1:
            kind, valid, step = "row", nb, 1
        else:
            kind, valid, step = "col", TOPK - ra, TOPK - ra
        rows = -(-valid // SUBLANES) * SUBLANES
        groups.append((kind, ra, valid, rows, row0))
        ra += step
        row0 += rows
    return groups


_CAND_GROUPS = _candidate_layout()
_BIG = float(1 << 20)


def _top16(s):
    nb, n = s.shape[0] // SUBLANES, s.shape[1]
    blocks = [s[SUBLANES * v:SUBLANES * (v + 1), :] for v in range(nb)]
    sub = lax.broadcasted_iota(jnp.int32, (SUBLANES, n), 0).astype(F32)
    vals, pos = [], []
    for _ in range(TOPK):
        best = blocks[0]
        first = jnp.zeros((SUBLANES, n), F32)
        for v in range(1, nb):
            first = jnp.where(blocks[v] > best, float(v), first)
            best = jnp.maximum(best, blocks[v])
        m = jnp.max(best, axis=0, keepdims=True)
        p = jnp.min(jnp.where(best == m, first * float(SUBLANES) + sub, _BIG), axis=0, keepdims=True)
        off = p - sub
        blocks = [jnp.where(off == float(SUBLANES * v), -jnp.inf, blocks[v]) for v in range(nb)]
        vals.append(m)
        pos.append(p)
    return jnp.concatenate(vals, axis=0), jnp.concatenate(pos, axis=0)


def _pick_rank(table, rk):
    out = jnp.zeros_like(table)
    for r in range(TOPK):
        out = jnp.where(rk == r, table[r:r + 1, :], out)
    return out


def _batcher_pairs(n):
    pairs = []
    p = 1
    while p < n:
        k = p
        while k >= 1:
            for j in range(k % p, n - k, 2 * k):
                for i in range(min(k, n - j - k)):
                    if (i + j) // (2 * p) == (i + j + k) // (2 * p):
                        pairs.append((i + j, i + j + k))
            k //= 2
        p *= 2
    return pairs


_SORT_PAIRS = _batcher_pairs(N_KEYS // SUBLANES)
assert TOPK & (TOPK - 1) == 0


def _merge_top(groups, n_out):
    out_v, out_i = [], []
    for r in range(n_out):
        heads = [(g[0][0], g[1][0]) for g in groups]
        top = heads[0][0]
        for hv, _ in heads[1:]:
            top = jnp.maximum(top, hv)
        m = jnp.max(top, axis=0, keepdims=True)
        cand = None
        for hv, hi in heads:
            c = jnp.where(hv == m, hi, _BIG)
            cand = c if cand is None else jnp.minimum(cand, c)
        p = jnp.min(cand, axis=0, keepdims=True)
        out_v.append(m)
        out_i.append(p)
        left = n_out - 1 - r
        if left == 0:
            break
        for vals, ids in groups:
            hit = ids[0] == p
            depth = len(vals)
            for d in range(min(depth - 1, left)):
                vals[d] = jnp.where(hit, vals[d + 1], vals[d])
                ids[d] = jnp.where(hit, ids[d + 1], ids[d])
            if depth - 1 < left:
                vals[depth - 1] = jnp.where(hit, -jnp.inf, vals[depth - 1])
    return jnp.concatenate(out_v, axis=0), jnp.concatenate(out_i, axis=0), out_v


def _top16_sorted(s):
    nb, n = s.shape[0] // SUBLANES, s.shape[1]
    assert nb == N_KEYS // SUBLANES
    sub = lax.broadcasted_iota(jnp.int32, (SUBLANES, n), 0).astype(F32)
    vals = [s[SUBLANES * v:SUBLANES * (v + 1), :] for v in range(nb)]
    ids = [sub + float(SUBLANES * v) for v in range(nb)]
    for i, j in _SORT_PAIRS:
        swap = vals[j] > vals[i]
        vals[i], vals[j] = jnp.maximum(vals[i], vals[j]), jnp.minimum(vals[i], vals[j])
        ids[i], ids[j] = jnp.where(swap, ids[j], ids[i]), jnp.where(swap, ids[i], ids[j])
    top_v, top_i, rows = _merge_top([[vals, ids]], TOPK + 1)
    ok = top_v[0:TOPK, :] > jnp.concatenate(rows[1:], axis=0)
    return top_v[0:TOPK, :], top_i[0:TOPK, :], ok


def _route_head_fast(st):
    n = st.shape[1]
    s1, i1, ok1 = _top16_sorted(st[0:N_KEYS, :])
    s2, i2, ok2 = _top16_sorted(st[N_KEYS:2 * N_KEYS, :])
    sub = lax.broadcasted_iota(jnp.int32, (SUBLANES, n), 0)
    sub_f = sub.astype(F32)
    head = s1[0:1, :] + s2
    lo_v, lo_i = [head[0:SUBLANES, :]], [sub_f]
    for ra in range(1, TOPK):
        nb = TOPK // (ra + 1)
        v = s1[ra:ra + 1, :] + s2[0:SUBLANES, :]
        c = sub_f + float(ra * TOPK)
        if nb < SUBLANES:
            v = jnp.where(sub < nb, v, -jnp.inf)
            c = jnp.where(sub < nb, c, _BIG)
        lo_v.append(v)
        lo_i.append(c)
    hi = [[head[SUBLANES:TOPK, :]], [sub_f + float(SUBLANES)]]
    best_s, best_c, _ = _merge_top([[lo_v, lo_i], hi], TOPK)
    best_c = best_c.astype(jnp.int32)
    ra_sel = lax.shift_right_logical(best_c, TOPK.bit_length() - 1).astype(F32)
    rb_sel = jnp.bitwise_and(best_c, TOPK - 1).astype(F32)
    e = jnp.exp(best_s - best_s[0:1, :])
    g = e / jnp.sum(e, axis=0, keepdims=True)
    ok = jnp.where(ok1, 0.0, 1.0) + jnp.where(ok2, 0.0, 1.0)
    return _pick_rank(i1, ra_sel), _pick_rank(i2, rb_sel), g, jnp.max(ok) == 0.0


def _route_head(st):
    n = st.shape[1]
    s1, i1 = _top16(st[0:N_KEYS, :])
    s2, i2 = _top16(st[N_KEYS:2 * N_KEYS, :])
    sub = lax.broadcasted_iota(jnp.int32, (SUBLANES, n), 0)
    parts = []
    for kind, ra, valid, rows, _ in _CAND_GROUPS:
        if kind == "row":
            part = s1[ra:ra + 1, :] + s2[0:rows, :]
        else:
            part = s1[ra:ra + rows, :] + s2[0:1, :]
        if valid < rows:
            assert rows == SUBLANES
            part = jnp.where(sub < valid, part, -jnp.inf)
        parts.append(part)
    best_s, best_p = _top16(jnp.concatenate(parts, axis=0))
    ra_sel = jnp.zeros_like(best_p)
    rb_sel = jnp.zeros_like(best_p)
    for kind, ra, valid, rows, row0 in _CAND_GROUPS:
        in_group = best_p >= float(row0)
        local = best_p - float(row0)
        if kind == "row":
            ra_sel = jnp.where(in_group, float(ra), ra_sel)
            rb_sel = jnp.where(in_group, local, rb_sel)
        else:
            ra_sel = jnp.where(in_group, local + float(ra), ra_sel)
            rb_sel = jnp.where(in_group, 0.0, rb_sel)
    e = jnp.exp(best_s - best_s[0:1, :])
    g = e / jnp.sum(e, axis=0, keepdims=True)
    return _pick_rank(i1, ra_sel), _pick_rank(i2, rb_sel), g


def _route_kernel(h_ref, ws_ref, a_ref, b_ref, g_ref):
    st = _dot_nt(ws_ref[...], h_ref[...])
    a_all, b_all, g_all = [], [], []
    for hd in range(N_HEADS):
        a, b, g = _route_head(st[hd * 2 * N_KEYS:(hd + 1) * 2 * N_KEYS, :])
        a_all.append(a)
        b_all.append(b)
        g_all.append(g)
    a_ref[...] = jnp.concatenate(a_all, axis=0).T.astype(jnp.int32)
    b_ref[...] = jnp.concatenate(b_all, axis=0).T.astype(jnp.int32)
    g_ref[...] = jnp.concatenate(g_all, axis=0).T


def _route(h2, ws):
    t, d = h2.shape
    tt = min(ROUTE_LANES, t)
    assert t % tt == 0
    hk = N_HEADS * TOPK
    out = jax.ShapeDtypeStruct((t, hk), jnp.int32)
    return pl.pallas_call(
        _route_kernel,
        grid=(t // tt,),
        in_specs=[pl.BlockSpec((tt, d), lambda i: (i, 0)),
                  pl.BlockSpec(ws.shape, lambda i: (0, 0))],
        out_specs=[pl.BlockSpec((tt, hk), lambda i: (i, 0))] * 3,
        out_shape=[out, out, jax.ShapeDtypeStruct((t, hk), F32)],
        compiler_params=pltpu.CompilerParams(
            dimension_semantics=("arbitrary",), vmem_limit_bytes=VMEM_LIMIT_BYTES),
        name="route",
    )(h2, ws)


def _experts_kernel(h_ref, hn_ref, ws_ref, a0_ref, b0_ref, g0_ref, u_ref, v_ref,
                    x1_ref, ada_ref, fg_ref, o_ref,
                    w3_ref, acc_ref, coef_ref, a_ref, b_ref, g_ref, at_ref, bt_ref, gt_ref, stage_ref,
                    *, tt, ec, n_chunks, n_tiles, final_norm):
    s = pl.program_id(0)
    i = s // n_chunks
    j = s % n_chunks
    rows_per_step = ec // N_KEYS
    half = tt // 2

    @pl.when(s == 0)
    def _():
        a_ref[...] = a0_ref[...]
        b_ref[...] = b0_ref[...]
        g_ref[...] = g0_ref[...]
        acc_ref[...] = jnp.zeros_like(acc_ref)
        coef_ref[...] = jnp.zeros_like(coef_ref)

    @pl.when((j == 0) & (i < n_tiles))
    def _():
        sub = lax.broadcasted_iota(jnp.int32, (N_KEYS, a_ref.shape[1]), 0)

        def token_weights(t):
            arow = a_ref[pl.ds(t, 1), :]
            brow = b_ref[pl.ds(t, 1), :]
            grow = GELU_FOLD * g_ref[pl.ds(t, 1), :]
            pt = jnp.where(sub == arow, grow, 0.0).astype(BF16)
            qt = jnp.where(sub == brow, 1.0, 0.0).astype(BF16)
            return _dot_nt(pt, qt)

        pairs_per_trip = BUILD_UNROLL // 2

        def build(trip, carry):
            base = pl.multiple_of(trip * (pairs_per_trip * W_PITCH), SUBLANES)
            for q in range(pairs_per_trip):
                p = trip * pairs_per_trip + q
                stage = stage_ref.at[q % STAGE_SLOTS]
                stage[pl.ds(0, N_KEYS, stride=2), :] = token_weights(2 * p)
                stage[pl.ds(1, N_KEYS, stride=2), :] = token_weights(2 * p + 1)
                w3_ref[pl.ds(base + q * W_PITCH, N_KEYS), :] = pltpu.bitcast(
                    stage[...].astype(BF16), jnp.uint32)
            return carry

        lax.fori_loop(0, tt // BUILD_UNROLL, build, 0)

    upc = 2 * N_HEADS // n_chunks
    tok_half = (j * upc) // N_HEADS
    hd0 = (j * upc) % N_HEADS
    hn_rows = pl.ds(pl.multiple_of(tok_half * half, half), half)
    scores = _dot_nt(ws_ref[...], hn_ref[hn_rows, :])

    acc_ref[...] += _dot(coef_ref[...], v_ref[...])

    act = _dot_nt(h_ref[...], u_ref[...])
    gel = act * (1.0 + lax.erf(act))
    parts = []
    for k in range(rows_per_step):
        word = w3_ref[pl.ds(j * rows_per_step + k, tt // 2, stride=W_PITCH), :]
        w_k = pltpu.bitcast(word, BF16)
        parts.append(w_k * gel[:, k * N_KEYS:(k + 1) * N_KEYS].astype(BF16))
    coef_ref[...] = jnp.concatenate(parts, axis=1)

    all_ok = None
    for q in range(upc):
        ra, rb, rg, fast_ok = _route_head_fast(scores[q * 2 * N_KEYS:(q + 1) * 2 * N_KEYS, :])
        at_ref[tok_half, hd0 + q] = ra
        bt_ref[tok_half, hd0 + q] = rb
        gt_ref[tok_half, hd0 + q] = rg
        all_ok = fast_ok if all_ok is None else jnp.logical_and(all_ok, fast_ok)

    @pl.when(jnp.logical_not(all_ok))
    def _():
        exact_scores = _dot_nt(ws_ref[...], hn_ref[hn_rows, :])
        for q in range(upc):
            ra, rb, rg = _route_head(exact_scores[q * 2 * N_KEYS:(q + 1) * 2 * N_KEYS, :])
            at_ref[tok_half, hd0 + q] = ra
            bt_ref[tok_half, hd0 + q] = rb
            gt_ref[tok_half, hd0 + q] = rg

    @pl.when((j == 0) & (s > 0))
    def _():
        g2 = ada_ref[0][5:6]
        x2 = x1_ref[...] + g2 * acc_ref[...]
        if final_norm:
            x2 = _rms(x2, fg_ref[...])
        o_ref[...] = x2
        acc_ref[...] = jnp.zeros_like(acc_ref)

    @pl.when(j == n_chunks - 1)
    def _():
        hk = N_HEADS * TOPK
        for hf in range(2):
            rows = slice(hf * half, (hf + 1) * half)
            a_ref[rows, :] = at_ref[hf].reshape(hk, half).T.astype(jnp.int32)
            b_ref[rows, :] = bt_ref[hf].reshape(hk, half).T.astype(jnp.int32)
            g_ref[rows, :] = gt_ref[hf].reshape(hk, half).T


def _experts(h2, ws, abg0, u, v, x1, ada, final_g, seq, final_norm):
    t, d = h2.shape
    ne = u.shape[0]
    hk = N_HEADS * TOPK
    tt = min(EXPERT_TOKENS, seq)
    ec = EXPERT_CHUNK
    half = tt // 2
    n_tiles = t // tt
    nc = ne // ec
    upc = 2 * N_HEADS // nc
    assert t % tt == 0 and seq % tt == 0 and ne == N_KEYS * N_KEYS and ne % ec == 0
    assert ec % (2 * N_KEYS) == 0 and upc * nc == 2 * N_HEADS and N_HEADS % upc == 0
    assert tt % BUILD_UNROLL == 0
    kern = functools.partial(_experts_kernel, tt=tt, ec=ec, n_chunks=nc, n_tiles=n_tiles,
                             final_norm=final_norm)
    tile = lambda s: jnp.minimum(s // nc, n_tiles - 1)
    done = lambda s: jnp.maximum(s - 1, 0) // nc
    first = lambda w: pl.BlockSpec((tt, w), lambda s: (0, 0), pipeline_mode=pl.Buffered(1))
    scr_t = pltpu.VMEM((2, N_HEADS, TOPK, half), F32)
    return pl.pallas_call(
        kern,
        grid=(n_tiles * nc + 1,),
        in_specs=[pl.BlockSpec((tt, d), lambda s: (tile(s), 0)),
                  pl.BlockSpec((tt, d), lambda s: (jnp.minimum(s // nc + 1, n_tiles - 1), 0)),
                  pl.BlockSpec((upc * 2 * N_KEYS, d), lambda s: (s % (N_HEADS // upc), 0)),
                  first(hk), first(hk), first(hk),
                  pl.BlockSpec((ec, d), lambda s: (s % nc, 0)),
                  pl.BlockSpec((ec, d), lambda s: ((s + nc - 1) % nc, 0)),
                  pl.BlockSpec((tt, d), lambda s: (done(s), 0)),
                  pl.BlockSpec((1,) + ada.shape[1:], lambda s: ((done(s) * tt) // seq, 0, 0)),
                  pl.BlockSpec(final_g.shape, lambda s: (0, 0))],
        out_specs=pl.BlockSpec((tt, d), lambda s: (done(s), 0)),
        out_shape=jax.ShapeDtypeStruct((t, d), F32),
        scratch_shapes=[pltpu.VMEM((tt // 2 * W_PITCH, N_KEYS), jnp.uint32),
                        pltpu.VMEM((tt, d), F32),
                        pltpu.VMEM((tt, ec), BF16),
                        pltpu.VMEM((tt, hk), jnp.int32),
                        pltpu.VMEM((tt, hk), jnp.int32),
                        pltpu.VMEM((tt, hk), F32),
                        scr_t, scr_t, scr_t,
                        pltpu.VMEM((STAGE_SLOTS, 2 * N_KEYS, N_KEYS), F32)],
        compiler_params=pltpu.CompilerParams(
            dimension_semantics=("arbitrary",),
            vmem_limit_bytes=VMEM_LIMIT_BYTES),
        name="experts",
    )(h2, h2, ws, *abg0, u, v, x1, ada, final_g)


def _block_diag(w):
    nh, hd, _ = w.shape
    eye = jnp.eye(nh, dtype=w.dtype)
    return (eye[:, None, :, None] * w[:, :, None, :]).reshape(nh * hd, nh * hd)


def kernel(x, c, w_ada, b_ada, norm1_g, w_in, conv_a_w, conv_b_w, conv_b_b, w_r, b_r, w_i, b_i,
           lru_lambda, gn_a, gn_b, w_out, norm2_g, w_q, sub_keys, expert_u, expert_v, final_g):
    bsz, seq, d = x.shape
    depth = w_ada.shape[0]
    cw = conv_a_w.shape[2]
    hd = cw // N_HEADS
    head_mean = _block_diag(jnp.full((N_HEADS, hd, hd), 1.0 / hd, F32)).astype(BF16)
    row = lambda p: p.reshape(1, -1)
    for l in range(depth):
        ada = _adaln(c, w_ada[l], b_ada[l]).reshape(bsz, 6, d)
        w_gate = jnp.concatenate([_block_diag(w_r[l]), _block_diag(w_i[l])], axis=1).astype(BF16)
        b_gate = jnp.concatenate([row(b_r[l]), row(b_i[l])], axis=1)
        x1, h2 = _mixer(x, ada, row(norm1_g[l]), w_in[l].astype(BF16), conv_a_w[l], conv_b_w[l],
                        row(conv_b_b[l]), w_gate, b_gate, row(lru_lambda[l]), row(gn_a[l]),
                        row(gn_b[l]), head_mean, w_out[l].astype(BF16), row(norm2_g[l]))
        dk = sub_keys.shape[-1]
        keys = sub_keys[l].reshape(N_HEADS * 2, N_KEYS, dk)
        wq_t = w_q[l].T.reshape(N_HEADS * 2, dk, d)
        ws = _keyfold(keys, wq_t).reshape(N_HEADS * 2 * N_KEYS, d)
        h2f = h2.reshape(bsz * seq, d)
        abg0 = _route(h2f[:min(EXPERT_TOKENS, seq)], ws)
        x = _experts(h2f, ws, abg0, (expert_u[l] * RSQRT2).astype(BF16), expert_v[l].astype(BF16),
                     x1.reshape(bsz * seq, d), ada, row(final_g), seq,
                     final_norm=(l == depth - 1)).reshape(bsz, seq, d)
    return x
```

```python
import functools

import numpy as np
import jax
import jax.numpy as jnp
from jax import lax
from jax.experimental import pallas as pl
from jax.experimental.pallas import tpu as pltpu

EPS = 1e-6
LRU_C = 8.0
N_HEADS = 8
TOPK = 16
N_KEYS = 128
SHORT_K = 3
LRU_K = 4
HALO = 8

SUBLANES = 8
LANES = 128
VMEM_LIMIT_BYTES = 56 * 1024 * 1024

W_PITCH = N_KEYS + SUBLANES // 2
BUILD_UNROLL = 128
MIXER_ROWS = 512
EXPERT_TOKENS = 512
EXPERT_CHUNK = 1024
ROUTE_LANES = 2 * LANES
RSQRT2 = float(1.0 / np.sqrt(2.0))
GELU_FOLD = RSQRT2

F32 = jnp.float32
BF16 = jnp.bfloat16
NT_DIMS = (((1,), (1,)), ((), ()))


def _dot(a, b):
    return jnp.dot(a, b, preferred_element_type=F32)


def _dot_nt(a, b):
    return lax.dot_general(a, b, NT_DIMS, preferred_element_type=F32)


def _rms(x, g):
    return x * lax.rsqrt(jnp.mean(x * x, axis=-1, keepdims=True) + EPS) * g


def _adaln_kernel(c_ref, w_ref, b_ref, o_ref):
    c = c_ref[...]
    c_act = c * jax.nn.sigmoid(c)
    o_ref[...] = jnp.dot(c_act, w_ref[...], preferred_element_type=F32,
                         precision=lax.Precision.HIGHEST) + b_ref[...]


def _adaln(c, w, b):
    bsz, d = c.shape
    n = w.shape[1]
    blk = d
    return pl.pallas_call(
        _adaln_kernel,
        grid=(n // blk,),
        in_specs=[pl.BlockSpec((bsz, d), lambda j: (0, 0)),
                  pl.BlockSpec((d, blk), lambda j: (0, j)),
                  pl.BlockSpec((1, blk), lambda j: (0, j))],
        out_specs=pl.BlockSpec((bsz, blk), lambda j: (0, j)),
        out_shape=jax.ShapeDtypeStruct((bsz, n), F32),
        compiler_params=pltpu.CompilerParams(vmem_limit_bytes=VMEM_LIMIT_BYTES),
        name="adaln",
    )(c, w, b.reshape(1, n))


def _causal_conv(buf_ref, v, w_ref, ts, k_w):
    buf_ref[HALO:HALO + ts, :] = v
    out = v * w_ref[k_w - 1:k_w, :]
    for k in range(k_w - 1):
        shift = k_w - 1 - k
        out = out + buf_ref[HALO - shift:HALO - shift + ts, :] * w_ref[k:k + 1, :]
    buf_ref[0:HALO, :] = v[ts - HALO:ts, :]
    return out


def _head_ms(y, m_ref):
    return _dot((y * y).astype(BF16), m_ref[...])


def _mixer_kernel(x_ref, ada_ref, n1g_ref, win_ref, caw_ref, cbw_ref, cbb_ref, wg_ref, bg_ref,
                  lam_ref, gna_ref, gnb_ref, hm_ref, wout_ref, n2g_ref,
                  x1_ref, h2_ref, bufa_ref, bufb_ref, carry_ref, *, ts, cw, rows):
    @pl.when(pl.program_id(1) == 0)
    def _():
        bufa_ref[:, 0:HALO, :] = jnp.zeros((rows, HALO, cw), F32)
        bufb_ref[:, 0:HALO, :] = jnp.zeros((rows, HALO, cw), F32)
        carry_ref[...] = jnp.zeros_like(carry_ref)

    for c in range(rows):
        _mixer_chain(x_ref.at[c], ada_ref.at[c], n1g_ref, win_ref, caw_ref, cbw_ref, cbb_ref,
                     wg_ref, bg_ref, lam_ref, gna_ref, gnb_ref, hm_ref, wout_ref, n2g_ref,
                     x1_ref.at[c], h2_ref.at[c], bufa_ref.at[c], bufb_ref.at[c], carry_ref.at[c],
                     ts=ts, cw=cw)


def _mixer_chain(x_ref, ada_ref, n1g_ref, win_ref, caw_ref, cbw_ref, cbb_ref, wg_ref, bg_ref,
                 lam_ref, gna_ref, gnb_ref, hm_ref, wout_ref, n2g_ref,
                 x1_ref, h2_ref, bufa_ref, bufb_ref, carry_ref, *, ts, cw):
    x = x_ref[...]
    ada = ada_ref[...]
    sh1, sc1, g1 = ada[0:1], ada[1:2], ada[2:3]
    sh2, sc2 = ada[3:4], ada[4:5]

    h = _rms(x, n1g_ref[...] * (1.0 + sc1)) + sh1
    z = _dot(h.astype(BF16), win_ref[...])
    gate_b = z[:, 0:cw]
    gate_c = z[:, cw:2 * cw]
    xa = z[:, 2 * cw:3 * cw]
    xr = z[:, 3 * cw:4 * cw]
    gr = z[:, 4 * cw:5 * cw]

    y_a = gate_b * _causal_conv(bufa_ref, gate_c * xa, caw_ref, ts, SHORT_K)

    xc = _causal_conv(bufb_ref, xr, cbw_ref, ts, LRU_K) + cbb_ref[...]
    pre = _dot(xc.astype(BF16), wg_ref[...]) + bg_ref[...]
    r = jax.nn.sigmoid(pre[:, 0:cw])
    i = jax.nn.sigmoid(pre[:, cw:2 * cw])
    nl = -lam_ref[...]
    softplus = jnp.maximum(nl, 0.0) + jnp.log1p(jnp.exp(-jnp.abs(nl)))
    log_a = (-LRU_C) * r * softplus
    a = jnp.exp(log_a)
    u = jnp.sqrt(-jnp.tanh(log_a) * (a * a + 1.0)) * (i * xc)

    ng = ts // SUBLANES
    acc_a = a.reshape(ng, SUBLANES, cw)
    acc_b = u.reshape(ng, SUBLANES, cw)
    sub = lax.broadcasted_iota(jnp.int32, (ng, SUBLANES, cw), 1)
    d = 1
    while d < SUBLANES:
        keep = sub >= d
        a_sh = jnp.where(keep, pltpu.roll(acc_a, d, 1), 1.0)
        b_sh = jnp.where(keep, pltpu.roll(acc_b, d, 1), 0.0)
        acc_b = acc_a * b_sh + acc_b
        acc_a = acc_a * a_sh
        d *= 2
    state = carry_ref[0:1, :]
    groups = []
    for g in range(ng):
        hg = acc_a[g] * state + acc_b[g]
        groups.append(hg)
        state = hg[SUBLANES - 1:SUBLANES, :]
    hseq = jnp.concatenate(groups, axis=0)
    carry_ref[0:1, :] = state

    y_b = hseq * jax.nn.gelu(gr, approximate=True)

    na = y_a * lax.rsqrt(_head_ms(y_a, hm_ref) + EPS) * gna_ref[...]
    nb = y_b * lax.rsqrt(_head_ms(y_b, hm_ref) + EPS) * gnb_ref[...]
    y = _dot(na.astype(BF16), wout_ref[0:cw, :]) + _dot(nb.astype(BF16), wout_ref[cw:2 * cw, :])

    x1 = x + g1 * y
    x1_ref[...] = x1
    h2_ref[...] = (_rms(x1, n2g_ref[...] * (1.0 + sc2)) + sh2).astype(BF16)


def _mixer(x, ada, n1g, w_in, caw, cbw, cbb, w_gate, b_gate, lam, gna, gnb, head_mean, w_out, n2g):
    bsz, seq, d = x.shape
    cw = caw.shape[1]
    ts = min(MIXER_ROWS, seq)
    rows = 2 if bsz % 2 == 0 else 1
    assert seq % ts == 0 and ts % SUBLANES == 0 and ts >= HALO
    full = lambda a: pl.BlockSpec(a.shape, lambda b, s: (0,) * a.ndim)
    kern = functools.partial(_mixer_kernel, ts=ts, cw=cw, rows=rows)
    return pl.pallas_call(
        kern,
        grid=(bsz // rows, seq // ts),
        in_specs=[pl.BlockSpec((rows, ts, d), lambda b, s: (b, s, 0)),
                  pl.BlockSpec((rows,) + ada.shape[1:], lambda b, s: (b, 0, 0)),
                  full(n1g), full(w_in), full(caw), full(cbw), full(cbb), full(w_gate),
                  full(b_gate), full(lam), full(gna), full(gnb), full(head_mean), full(w_out),
                  full(n2g)],
        out_specs=[pl.BlockSpec((rows, ts, d), lambda b, s: (b, s, 0)),
                   pl.BlockSpec((rows, ts, d), lambda b, s: (b, s, 0))],
        out_shape=[jax.ShapeDtypeStruct((bsz, seq, d), F32),
                   jax.ShapeDtypeStruct((bsz, seq, d), BF16)],
        scratch_shapes=[pltpu.VMEM((rows, ts + HALO, cw), F32),
                        pltpu.VMEM((rows, ts + HALO, cw), F32),
                        pltpu.VMEM((rows, SUBLANES, cw), F32)],
        compiler_params=pltpu.CompilerParams(
            dimension_semantics=("arbitrary", "arbitrary"),
            vmem_limit_bytes=VMEM_LIMIT_BYTES),
        name="mixer",
    )(x, ada, n1g, w_in, caw, cbw, cbb, w_gate, b_gate, lam, gna, gnb, head_mean, w_out, n2g)


def _keyfold_kernel(k_ref, wqt_ref, o_ref):
    o_ref[0] = jnp.dot(k_ref[0], wqt_ref[0], preferred_element_type=F32,
                       precision=lax.Precision.HIGHEST).astype(BF16)


def _keyfold(keys, wq_t):
    g, n, dk = keys.shape
    d = wq_t.shape[2]
    return pl.pallas_call(
        _keyfold_kernel,
        grid=(g,),
        in_specs=[pl.BlockSpec((1, n, dk), lambda i: (i, 0, 0)),
                  pl.BlockSpec((1, dk, d), lambda i: (i, 0, 0))],
        out_specs=pl.BlockSpec((1, n, d), lambda i: (i, 0, 0)),
        out_shape=jax.ShapeDtypeStruct((g, n, d), BF16),
        compiler_params=pltpu.CompilerParams(vmem_limit_bytes=VMEM_LIMIT_BYTES),
        name="keyfold",
    )(keys, wq_t)


def _candidate_layout():
    groups = []
    ra = row0 = 0
    while ra < TOPK:
        nb = TOPK // (ra + 1)
        if nb > 1:
            kind, valid, step = "row", nb, 1
        else:
            kind, valid, step = "col", TOPK - ra, TOPK - ra
        rows = -(-valid // SUBLANES) * SUBLANES
        groups.append((kind, ra, valid, rows, row0))
        ra += step
        row0 += rows
    return groups


_CAND_GROUPS = _candidate_layout()
_BIG = float(1 << 20)


def _top16(s):
    nb, n = s.shape[0] // SUBLANES, s.shape[1]
    blocks = [s[SUBLANES * v:SUBLANES * (v + 1), :] for v in range(nb)]
    sub = lax.broadcasted_iota(jnp.int32, (SUBLANES, n), 0).astype(F32)
    vals, pos = [], []
    for _ in range(TOPK):
        best = blocks[0]
        first = jnp.zeros((SUBLANES, n), F32)
        for v in range(1, nb):
            first = jnp.where(blocks[v] > best, float(v), first)
            best = jnp.maximum(best, blocks[v])
        m = jnp.max(best, axis=0, keepdims=True)
        p = jnp.min(jnp.where(best == m, first * float(SUBLANES) + sub, _BIG), axis=0, keepdims=True)
        off = p - sub
        blocks = [jnp.where(off == float(SUBLANES * v), -jnp.inf, blocks[v]) for v in range(nb)]
        vals.append(m)
        pos.append(p)
    return jnp.concatenate(vals, axis=0), jnp.concatenate(pos, axis=0)


def _pick_rank(table, rk):
    out = jnp.zeros_like(table)
    for r in range(TOPK):
        out = jnp.where(rk == r, table[r:r + 1, :], out)
    return out


def _batcher_pairs(n):
    pairs = []
    p = 1
    while p < n:
        k = p
        while k >= 1:
            for j in range(k % p, n - k, 2 * k):
                for i in range(min(k, n - j - k)):
                    if (i + j) // (2 * p) == (i + j + k) // (2 * p):
                        pairs.append((i + j, i + j + k))
            k //= 2
        p *= 2
    return pairs


_SORT_PAIRS = _batcher_pairs(N_KEYS // SUBLANES)
assert TOPK & (TOPK - 1) == 0


def _merge_top(groups, n_out):
    out_v, out_i = [], []
    for r in range(n_out):
        heads = [(g[0][0], g[1][0]) for g in groups]
        top = heads[0][0]
        for hv, _ in heads[1:]:
            top = jnp.maximum(top, hv)
        m = jnp.max(top, axis=0, keepdims=True)
        cand = None
        for hv, hi in heads:
            c = jnp.where(hv == m, hi, _BIG)
            cand = c if cand is None else jnp.minimum(cand, c)
        p = jnp.min(cand, axis=0, keepdims=True)
        out_v.append(m)
        out_i.append(p)
        left = n_out - 1 - r
        if left == 0:
            break
        for vals, ids in groups:
            hit = ids[0] == p
            depth = len(vals)
            for d in range(min(depth - 1, left)):
                vals[d] = jnp.where(hit, vals[d + 1], vals[d])
                ids[d] = jnp.where(hit, ids[d + 1], ids[d])
            if depth - 1 < left:
                vals[depth - 1] = jnp.where(hit, -jnp.inf, vals[depth - 1])
    return jnp.concatenate(out_v, axis=0), jnp.concatenate(out_i, axis=0), out_v


def _top16_sorted(s):
    nb, n = s.shape[0] // SUBLANES, s.shape[1]
    assert nb == N_KEYS // SUBLANES
    sub = lax.broadcasted_iota(jnp.int32, (SUBLANES, n), 0).astype(F32)
    vals = [s[SUBLANES * v:SUBLANES * (v + 1), :] for v in range(nb)]
    ids = [sub + float(SUBLANES * v) for v in range(nb)]
    for i, j in _SORT_PAIRS:
        swap = vals[j] > vals[i]
        vals[i], vals[j] = jnp.maximum(vals[i], vals[j]), jnp.minimum(vals[i], vals[j])
        ids[i], ids[j] = jnp.where(swap, ids[j], ids[i]), jnp.where(swap, ids[i], ids[j])
    top_v, top_i, rows = _merge_top([[vals, ids]], TOPK + 1)
    ok = top_v[0:TOPK, :] > jnp.concatenate(rows[1:], axis=0)
    return top_v[0:TOPK, :], top_i[0:TOPK, :], ok


def _route_head_fast(st):
    n = st.shape[1]
    s1, i1, ok1 = _top16_sorted(st[0:N_KEYS, :])
    s2, i2, ok2 = _top16_sorted(st[N_KEYS:2 * N_KEYS, :])
    sub = lax.broadcasted_iota(jnp.int32, (SUBLANES, n), 0)
    sub_f = sub.astype(F32)
    head = s1[0:1, :] + s2
    lo_v, lo_i = [head[0:SUBLANES, :]], [sub_f]
    for ra in range(1, TOPK):
        nb = TOPK // (ra + 1)
        v = s1[ra:ra + 1, :] + s2[0:SUBLANES, :]
        c = sub_f + float(ra * TOPK)
        if nb < SUBLANES:
            v = jnp.where(sub < nb, v, -jnp.inf)
            c = jnp.where(sub < nb, c, _BIG)
        lo_v.append(v)
        lo_i.append(c)
    hi = [[head[SUBLANES:TOPK, :]], [sub_f + float(SUBLANES)]]
    best_s, best_c, _ = _merge_top([[lo_v, lo_i], hi], TOPK)
    best_c = best_c.astype(jnp.int32)
    ra_sel = lax.shift_right_logical(best_c, TOPK.bit_length() - 1).astype(F32)
    rb_sel = jnp.bitwise_and(best_c, TOPK - 1).astype(F32)
    e = jnp.exp(best_s - best_s[0:1, :])
    g = e / jnp.sum(e, axis=0, keepdims=True)
    ok = jnp.where(ok1, 0.0, 1.0) + jnp.where(ok2, 0.0, 1.0)
    return _pick_rank(i1, ra_sel), _pick_rank(i2, rb_sel), g, jnp.max(ok) == 0.0


def _route_head(st):
    n = st.shape[1]
    s1, i1 = _top16(st[0:N_KEYS, :])
    s2, i2 = _top16(st[N_KEYS:2 * N_KEYS, :])
    sub = lax.broadcasted_iota(jnp.int32, (SUBLANES, n), 0)
    parts = []
    for kind, ra, valid, rows, _ in _CAND_GROUPS:
        if kind == "row":
            part = s1[ra:ra + 1, :] + s2[0:rows, :]
        else:
            part = s1[ra:ra + rows, :] + s2[0:1, :]
        if valid < rows:
            assert rows == SUBLANES
            part = jnp.where(sub < valid, part, -jnp.inf)
        parts.append(part)
    best_s, best_p = _top16(jnp.concatenate(parts, axis=0))
    ra_sel = jnp.zeros_like(best_p)
    rb_sel = jnp.zeros_like(best_p)
    for kind, ra, valid, rows, row0 in _CAND_GROUPS:
        in_group = best_p >= float(row0)
        local = best_p - float(row0)
        if kind == "row":
            ra_sel = jnp.where(in_group, float(ra), ra_sel)
            rb_sel = jnp.where(in_group, local, rb_sel)
        else:
            ra_sel = jnp.where(in_group, local + float(ra), ra_sel)
            rb_sel = jnp.where(in_group, 0.0, rb_sel)
    e = jnp.exp(best_s - best_s[0:1, :])
    g = e / jnp.sum(e, axis=0, keepdims=True)
    return _pick_rank(i1, ra_sel), _pick_rank(i2, rb_sel), g


def _route_kernel(h_ref, ws_ref, a_ref, b_ref, g_ref):
    st = _dot_nt(ws_ref[...], h_ref[...])
    a_all, b_all, g_all = [], [], []
    for hd in range(N_HEADS):
        a, b, g = _route_head(st[hd * 2 * N_KEYS:(hd + 1) * 2 * N_KEYS, :])
        a_all.append(a)
        b_all.append(b)
        g_all.append(g)
    a_ref[...] = jnp.concatenate(a_all, axis=0).T.astype(jnp.int32)
    b_ref[...] = jnp.concatenate(b_all, axis=0).T.astype(jnp.int32)
    g_ref[...] = jnp.concatenate(g_all, axis=0).T


def _route(h2, ws):
    t, d = h2.shape
    tt = min(ROUTE_LANES, t)
    assert t % tt == 0
    hk = N_HEADS * TOPK
    out = jax.ShapeDtypeStruct((t, hk), jnp.int32)
    return pl.pallas_call(
        _route_kernel,
        grid=(t // tt,),
        in_specs=[pl.BlockSpec((tt, d), lambda i: (i, 0)),
                  pl.BlockSpec(ws.shape, lambda i: (0, 0))],
        out_specs=[pl.BlockSpec((tt, hk), lambda i: (i, 0))] * 3,
        out_shape=[out, out, jax.ShapeDtypeStruct((t, hk), F32)],
        compiler_params=pltpu.CompilerParams(
            dimension_semantics=("arbitrary",), vmem_limit_bytes=VMEM_LIMIT_BYTES),
        name="route",
    )(h2, ws)


def _experts_kernel(h_ref, hn_ref, ws_ref, a0_ref, b0_ref, g0_ref, u_ref, v_ref,
                    x1_ref, ada_ref, fg_ref, o_ref,
                    w3_ref, acc_ref, coef_ref, a_ref, b_ref, g_ref, at_ref, bt_ref, gt_ref,
                    *, tt, ec, n_chunks, n_tiles, final_norm):
    s = pl.program_id(0)
    i = s // n_chunks
    j = s % n_chunks
    rows_per_step = ec // N_KEYS
    half = tt // 2

    @pl.when(s == 0)
    def _():
        a_ref[...] = a0_ref[...]
        b_ref[...] = b0_ref[...]
        g_ref[...] = g0_ref[...]
        acc_ref[...] = jnp.zeros_like(acc_ref)
        coef_ref[...] = jnp.zeros_like(coef_ref)

    @pl.when((j == 0) & (i < n_tiles))
    def _():
        sub = lax.broadcasted_iota(jnp.int32, (N_KEYS, a_ref.shape[1]), 0)

        sub2 = lax.broadcasted_iota(jnp.int32, (2 * N_KEYS, a_ref.shape[1]), 0)

        def pair_weights(p):
            lhs, rhs = [], []
            for e in range(2):
                t = 2 * p + e
                arow = a_ref[pl.ds(t, 1), :]
                brow = b_ref[pl.ds(t, 1), :]
                grow = GELU_FOLD * g_ref[pl.ds(t, 1), :]
                lhs.append(jnp.where(sub2 == 2 * arow + e, grow, 0.0))
                rhs.append(jnp.where(sub == brow, 1.0, 0.0))
            return _dot_nt(jnp.concatenate(lhs, axis=1).astype(BF16),
                           jnp.concatenate(rhs, axis=1).astype(BF16))

        pairs_per_trip = BUILD_UNROLL // 2

        def build(trip, carry):
            base = pl.multiple_of(trip * (pairs_per_trip * W_PITCH), SUBLANES)
            for q in range(pairs_per_trip):
                p = trip * pairs_per_trip + q
                w3_ref[pl.ds(base + q * W_PITCH, N_KEYS), :] = pltpu.bitcast(
                    pair_weights(p).astype(BF16), jnp.uint32)
            return carry

        lax.fori_loop(0, tt // BUILD_UNROLL, build, 0)

    upc = 2 * N_HEADS // n_chunks
    tok_half = (j * upc) // N_HEADS
    hd0 = (j * upc) % N_HEADS
    hn_rows = pl.ds(pl.multiple_of(tok_half * half, half), half)
    scores = _dot_nt(ws_ref[...], hn_ref[hn_rows, :])

    acc_ref[...] += _dot(coef_ref[...], v_ref[...])

    act = _dot_nt(h_ref[...], u_ref[...])
    gel = act * (1.0 + lax.erf(act))
    parts = []
    for k in range(rows_per_step):
        word = w3_ref[pl.ds(j * rows_per_step + k, tt // 2, stride=W_PITCH), :]
        w_k = pltpu.bitcast(word, BF16)
        parts.append(w_k * gel[:, k * N_KEYS:(k + 1) * N_KEYS].astype(BF16))
    coef_ref[...] = jnp.concatenate(parts, axis=1)

    all_ok = None
    for q in range(upc):
        ra, rb, rg, fast_ok = _route_head_fast(scores[q * 2 * N_KEYS:(q + 1) * 2 * N_KEYS, :])
        at_ref[tok_half, hd0 + q] = ra
        bt_ref[tok_half, hd0 + q] = rb
        gt_ref[tok_half, hd0 + q] = rg
        all_ok = fast_ok if all_ok is None else jnp.logical_and(all_ok, fast_ok)

    @pl.when(jnp.logical_not(all_ok))
    def _():
        exact_scores = _dot_nt(ws_ref[...], hn_ref[hn_rows, :])
        for q in range(upc):
            ra, rb, rg = _route_head(exact_scores[q * 2 * N_KEYS:(q + 1) * 2 * N_KEYS, :])
            at_ref[tok_half, hd0 + q] = ra
            bt_ref[tok_half, hd0 + q] = rb
            gt_ref[tok_half, hd0 + q] = rg

    @pl.when((j == 0) & (s > 0))
    def _():
        g2 = ada_ref[0][5:6]
        x2 = x1_ref[...] + g2 * acc_ref[...]
        if final_norm:
            x2 = _rms(x2, fg_ref[...])
        o_ref[...] = x2
        acc_ref[...] = jnp.zeros_like(acc_ref)

    @pl.when(j == n_chunks - 1)
    def _():
        hk = N_HEADS * TOPK
        for hf in range(2):
            rows = slice(hf * half, (hf + 1) * half)
            a_ref[rows, :] = at_ref[hf].reshape(hk, half).T.astype(jnp.int32)
            b_ref[rows, :] = bt_ref[hf].reshape(hk, half).T.astype(jnp.int32)
            g_ref[rows, :] = gt_ref[hf].reshape(hk, half).T


def _experts(h2, ws, abg0, u, v, x1, ada, final_g, seq, final_norm):
    t, d = h2.shape
    ne = u.shape[0]
    hk = N_HEADS * TOPK
    tt = min(EXPERT_TOKENS, seq)
    ec = EXPERT_CHUNK
    half = tt // 2
    n_tiles = t // tt
    nc = ne // ec
    upc = 2 * N_HEADS // nc
    assert t % tt == 0 and seq % tt == 0 and ne == N_KEYS * N_KEYS and ne % ec == 0
    assert ec % (2 * N_KEYS) == 0 and upc * nc == 2 * N_HEADS and N_HEADS % upc == 0
    assert tt % BUILD_UNROLL == 0
    kern = functools.partial(_experts_kernel, tt=tt, ec=ec, n_chunks=nc, n_tiles=n_tiles,
                             final_norm=final_norm)
    tile = lambda s: jnp.minimum(s // nc, n_tiles - 1)
    done = lambda s: jnp.maximum(s - 1, 0) // nc
    first = lambda w: pl.BlockSpec((tt, w), lambda s: (0, 0), pipeline_mode=pl.Buffered(1))
    scr_t = pltpu.VMEM((2, N_HEADS, TOPK, half), F32)
    return pl.pallas_call(
        kern,
        grid=(n_tiles * nc + 1,),
        in_specs=[pl.BlockSpec((tt, d), lambda s: (tile(s), 0)),
                  pl.BlockSpec((tt, d), lambda s: (jnp.minimum(s // nc + 1, n_tiles - 1), 0)),
                  pl.BlockSpec((upc * 2 * N_KEYS, d), lambda s: (s % (N_HEADS // upc), 0)),
                  first(hk), first(hk), first(hk),
                  pl.BlockSpec((ec, d), lambda s: (s % nc, 0)),
                  pl.BlockSpec((ec, d), lambda s: ((s + nc - 1) % nc, 0)),
                  pl.BlockSpec((tt, d), lambda s: (done(s), 0)),
                  pl.BlockSpec((1,) + ada.shape[1:], lambda s: ((done(s) * tt) // seq, 0, 0)),
                  pl.BlockSpec(final_g.shape, lambda s: (0, 0))],
        out_specs=pl.BlockSpec((tt, d), lambda s: (done(s), 0)),
        out_shape=jax.ShapeDtypeStruct((t, d), F32),
        scratch_shapes=[pltpu.VMEM((tt // 2 * W_PITCH, N_KEYS), jnp.uint32),
                        pltpu.VMEM((tt, d), F32),
                        pltpu.VMEM((tt, ec), BF16),
                        pltpu.VMEM((tt, hk), jnp.int32),
                        pltpu.VMEM((tt, hk), jnp.int32),
                        pltpu.VMEM((tt, hk), F32),
                        scr_t, scr_t, scr_t],
        compiler_params=pltpu.CompilerParams(
            dimension_semantics=("arbitrary",),
            vmem_limit_bytes=VMEM_LIMIT_BYTES),
        name="experts",
    )(h2, h2, ws, *abg0, u, v, x1, ada, final_g)


def _block_diag(w):
    nh, hd, _ = w.shape
    eye = jnp.eye(nh, dtype=w.dtype)
    return (eye[:, None, :, None] * w[:, :, None, :]).reshape(nh * hd, nh * hd)


def kernel(x, c, w_ada, b_ada, norm1_g, w_in, conv_a_w, conv_b_w, conv_b_b, w_r, b_r, w_i, b_i,
           lru_lambda, gn_a, gn_b, w_out, norm2_g, w_q, sub_keys, expert_u, expert_v, final_g):
    bsz, seq, d = x.shape
    depth = w_ada.shape[0]
    cw = conv_a_w.shape[2]
    hd = cw // N_HEADS
    head_mean = _block_diag(jnp.full((N_HEADS, hd, hd), 1.0 / hd, F32)).astype(BF16)
    row = lambda p: p.reshape(1, -1)
    for l in range(depth):
        ada = _adaln(c, w_ada[l], b_ada[l]).reshape(bsz, 6, d)
        w_gate = jnp.concatenate([_block_diag(w_r[l]), _block_diag(w_i[l])], axis=1).astype(BF16)
        b_gate = jnp.concatenate([row(b_r[l]), row(b_i[l])], axis=1)
        x1, h2 = _mixer(x, ada, row(norm1_g[l]), w_in[l].astype(BF16), conv_a_w[l], conv_b_w[l],
                        row(conv_b_b[l]), w_gate, b_gate, row(lru_lambda[l]), row(gn_a[l]),
                        row(gn_b[l]), head_mean, w_out[l].astype(BF16), row(norm2_g[l]))
        dk = sub_keys.shape[-1]
        keys = sub_keys[l].reshape(N_HEADS * 2, N_KEYS, dk)
        wq_t = w_q[l].T.reshape(N_HEADS * 2, dk, d)
        ws = _keyfold(keys, wq_t).reshape(N_HEADS * 2 * N_KEYS, d)
        h2f = h2.reshape(bsz * seq, d)
        abg0 = _route(h2f[:min(EXPERT_TOKENS, seq)], ws)
        x = _experts(h2f, ws, abg0, (expert_u[l] * RSQRT2).astype(BF16), expert_v[l].astype(BF16),
                     x1.reshape(bsz * seq, d), ada, row(final_g), seq,
                     final_norm=(l == depth - 1)).reshape(bsz, seq, d)
    return x
```
